```python
import jax, jax.numpy as jnp
from jax import lax
import numpy as np

D_MODEL = 2048
BATCH = 4
SEQ = 4096
DEPTH = 4

BRANCH_WIDTH = D_MODEL // 2
N_BRANCHES = 3
RET_HEADS = 4
RET_WIDTH = BRANCH_WIDTH
RET_HEAD_DIM = RET_WIDTH // RET_HEADS
RET_CHUNK = 128
RET_ROPE_BASE = 10000.0
POOL_WINDOWS = (2, 4, 8, 16)
POOL_GROUPS = len(POOL_WINDOWS)
POOL_WIDTH = BRANCH_WIDTH
POOL_GROUP_DIM = POOL_WIDTH // POOL_GROUPS
ATT_HEAD_DIM = 128
ATT_WIDTH = BRANCH_WIDTH
ATT_Q_HEADS = ATT_WIDTH // ATT_HEAD_DIM
ATT_KV_HEADS = ATT_Q_HEADS // 4
ATT_KV_WIDTH = ATT_KV_HEADS * ATT_HEAD_DIM
ATT_WINDOW = 128
ATT_BLOCK = 128
ROPE_THETA = 500000.0
ROPE_DIMS = ATT_HEAD_DIM // 4
RMS_EPS = 1e-6
NEG_BIG = -1e30

IN_SIZES = (RET_WIDTH, RET_WIDTH, RET_WIDTH, RET_WIDTH,
            POOL_WIDTH, POOL_WIDTH,
            ATT_WIDTH, ATT_KV_WIDTH, ATT_KV_WIDTH, ATT_WIDTH,
            N_BRANCHES * D_MODEL)
IN_WIDTH = sum(IN_SIZES)

kernel_name = "hybrid_retention_pool_swa_encoder"


def rms_norm(x, gain=None):
    x32 = x.astype(jnp.float32)
    y = x32 * lax.rsqrt(jnp.mean(x32 * x32, axis=-1, keepdims=True) + RMS_EPS)
    if gain is not None:
        y = y * gain.astype(jnp.float32)
    return y.astype(x.dtype)


def rotary(x, inv_freq):
    S = x.shape[1]
    half = inv_freq.shape[0]
    ang = jnp.arange(S, dtype=jnp.float32)[:, None] * inv_freq[None, :]
    cos = jnp.cos(ang)[None, :, None, :]
    sin = jnp.sin(ang)[None, :, None, :]
    xr = x[..., :2 * half].astype(jnp.float32)
    x1, x2 = xr[..., :half], xr[..., half:]
    rot = jnp.concatenate([x1 * cos - x2 * sin, x2 * cos + x1 * sin], axis=-1).astype(x.dtype)
    return jnp.concatenate([rot, x[..., 2 * half:]], axis=-1)


def retention(q, k, v, a_fwd, a_bwd):
    B, S, H, Dh = q.shape
    C = RET_CHUNK
    nC = S // C
    dt = q.dtype
    inv = 1.0 / (RET_ROPE_BASE ** jnp.linspace(0.0, 1.0, Dh // 2, dtype=jnp.float32))
    q = rotary(q, inv)
    k = rotary(k, inv) * jnp.asarray(Dh ** -0.5, dt)
    lg_f = -jnp.exp(a_fwd.astype(jnp.float32))
    lg_b = -jnp.exp(a_bwd.astype(jnp.float32))
    j = jnp.arange(C, dtype=jnp.float32)
    lag = j[:, None] - j[None, :]
    alag = jnp.abs(lag)[None]
    dmask = jnp.where(lag[None] >= 0,
                      jnp.exp(lg_f[:, None, None] * alag),
                      jnp.exp(lg_b[:, None, None] * alag)).astype(dt)
    qc = q.reshape(B, nC, C, H, Dh)
    kc = k.reshape(B, nC, C, H, Dh)
    vc = v.reshape(B, nC, C, H, Dh)
    scores = jnp.einsum('bnjhd,bnlhd->bnhjl', qc, kc) * dmask[None, None]
    out = jnp.einsum('bnhjl,bnlhe->bnjhe', scores, vc)
    w_f = jnp.exp(lg_f[None, :] * (C - 1 - j)[:, None]).astype(dt)
    w_b = jnp.exp(lg_b[None, :] * j[:, None]).astype(dt)
    kv_f = jnp.einsum('bnlhd,lh,bnlhe->nbhde', kc, w_f, vc)
    kv_b = jnp.einsum('bnlhd,lh,bnlhe->nbhde', kc, w_b, vc)
    dec_f = jnp.exp(lg_f * C).astype(dt)[None, :, None, None]
    dec_b = jnp.exp(lg_b * C).astype(dt)[None, :, None, None]

    def step_f(state, kv):
        return state * dec_f + kv, state

    def step_b(state, kv):
        return state * dec_b + kv, state

    init = jnp.zeros((B, H, Dh, Dh), dt)
    _, s_f = lax.scan(step_f, init, kv_f)
    _, s_b = lax.scan(step_b, init, kv_b, reverse=True)
    q_f = jnp.exp(lg_f[None, :] * (j + 1.0)[:, None]).astype(dt)
    q_b = jnp.exp(lg_b[None, :] * (C - j)[:, None]).astype(dt)
    out = (out
           + jnp.einsum('bnjhd,nbhde->bnjhe', qc * q_f[None, None, :, :, None], s_f)
           + jnp.einsum('bnjhd,nbhde->bnjhe', qc * q_b[None, None, :, :, None], s_b))
    out = rms_norm(out.reshape(B, S, H, Dh))
    return out.reshape(B, S, H * Dh)


def multiscale_pool(u, pool_w, pool_scale):
    B, S, _ = u.shape
    ug = u.reshape(B, S, POOL_GROUPS, POOL_GROUP_DIM).astype(jnp.float32)
    cs = jnp.pad(jnp.cumsum(ug, axis=1), ((0, 0), (1, 0), (0, 0), (0, 0)))
    pos = jnp.arange(S)
    groups = []
    for g, w in enumerate(POOL_WINDOWS):
        lo = jnp.clip(pos - w // 2, 0, S)
        hi = jnp.clip(pos + w // 2, 0, S)
        cnt = (hi - lo).astype(jnp.float32)[None, :, None]
        csg = cs[:, :, g]
        mean = (csg[:, hi] - csg[:, lo]) / cnt
        groups.append(mean - ug[:, :, g])
    p = jnp.stack(groups, axis=2).astype(u.dtype)
    y = jnp.einsum('bsgd,gde->bsge', p, pool_w).reshape(B, S, POOL_WIDTH)
    return y * pool_scale


def windowed_gqa(q, k, v, q_gain, k_gain, sink):
    B, S, Hq, Dh = q.shape
    Hkv = k.shape[2]
    G = Hq // Hkv
    nB = S // ATT_BLOCK
    L = ATT_BLOCK
    inv = ROPE_THETA ** (-jnp.arange(ROPE_DIMS // 2, dtype=jnp.float32) / (ROPE_DIMS // 2))
    q = rotary(rms_norm(q, q_gain), inv)
    k = rotary(rms_norm(k, k_gain), inv)
    qb = q.reshape(B, nB, L, Hkv, G, Dh)
    pad = ((0, 0), (1, 1), (0, 0), (0, 0), (0, 0))
    kp = jnp.pad(k.reshape(B, nB, L, Hkv, Dh), pad)
    vp = jnp.pad(v.reshape(B, nB, L, Hkv, Dh), pad)
    kw = jnp.concatenate([kp[:, :-2], kp[:, 1:-1], kp[:, 2:]], axis=2)
    vw = jnp.concatenate([vp[:, :-2], vp[:, 1:-1], vp[:, 2:]], axis=2)
    s = jnp.einsum('bnqkgd,bnskd->bnkgqs', qb, kw).astype(jnp.float32) * (Dh ** -0.5)
    blk = jnp.arange(nB)[:, None]
    qpos = blk * L + jnp.arange(L)[None, :]
    kpos = (blk - 1) * L + jnp.arange(3 * L)[None, :]
    diff = kpos[:, None, :] - qpos[:, :, None]
    valid = ((jnp.abs(diff) <= ATT_WINDOW)
             & (kpos >= 0)[:, None, :] & (kpos < S)[:, None, :])
    s = jnp.where(valid[None, :, None, None], s, NEG_BIG)
    sk = sink.astype(jnp.float32).reshape(Hkv, G)[None, None, :, :, None, None]
    m = jnp.maximum(jnp.max(s, axis=-1, keepdims=True), sk)
    p = jnp.exp(s - m)
    p = p / (jnp.sum(p, axis=-1, keepdims=True) + jnp.exp(sk - m))
    o = jnp.einsum('bnkgqs,bnskd->bnqkgd', p.astype(v.dtype), vw)
    return o.reshape(B, S, Hq * Dh)


def hybrid_layer(x, norm_g, w_in, a_fwd, a_bwd, pool_w, pool_scale,
                 q_gain, k_gain, sink, w_ret, w_pool, w_att, w_out):
    B, S, D = x.shape
    h = rms_norm(x, norm_g)
    z = jnp.einsum('bsd,de->bse', h, w_in)
    (rq, rk, rv, rg, pv, pg, aq, ak, av, ag, mg) = jnp.split(
        z, list(np.cumsum(IN_SIZES)[:-1]), axis=-1)
    ya = retention(rq.reshape(B, S, RET_HEADS, RET_HEAD_DIM),
                   rk.reshape(B, S, RET_HEADS, RET_HEAD_DIM),
                   rv.reshape(B, S, RET_HEADS, RET_HEAD_DIM), a_fwd, a_bwd)
    ya = jnp.einsum('bse,ed->bsd', ya * jax.nn.silu(rg), w_ret)
    yb = multiscale_pool(pv, pool_w, pool_scale)
    yb = jnp.einsum('bse,ed->bsd', yb * jax.nn.silu(pg), w_pool)
    yc = windowed_gqa(aq.reshape(B, S, ATT_Q_HEADS, ATT_HEAD_DIM),
                      ak.reshape(B, S, ATT_KV_HEADS, ATT_HEAD_DIM),
                      av.reshape(B, S, ATT_KV_HEADS, ATT_HEAD_DIM), q_gain, k_gain, sink)
    yc = jnp.einsum('bse,ed->bsd', yc * jax.nn.silu(ag), w_att)
    gates = jax.nn.sigmoid(mg.astype(jnp.float32)).astype(x.dtype).reshape(B, S, N_BRANCHES, D)
    merged = gates[:, :, 0] * ya + gates[:, :, 1] * yb + gates[:, :, 2] * yc
    return x + jnp.einsum('bsd,de->bse', merged, w_out)


def setup_inputs(seed: int = 0) -> dict:
    key = jax.random.key(seed)
    ks = jax.random.split(key, 14)
    f32 = jnp.float32
    D = D_MODEL
    nrm = jax.random.normal
    base = np.log(-np.log1p(-(2.0 ** (-5.0 - np.arange(RET_HEADS))))).astype(np.float32)
    base = jnp.asarray(base)[None, :]
    return {
        "x": nrm(ks[0], (BATCH, SEQ, D), f32),
        "norm_g": 1.0 + 0.02 * nrm(ks[1], (DEPTH, D), f32),
        "w_in": nrm(ks[2], (DEPTH, D, IN_WIDTH), f32) * (D ** -0.5),
        "ret_decay_fwd": base + 0.1 * nrm(ks[3], (DEPTH, RET_HEADS), f32),
        "ret_decay_bwd": base + 0.1 * nrm(ks[4], (DEPTH, RET_HEADS), f32),
        "pool_w": nrm(ks[5], (DEPTH, POOL_GROUPS, POOL_GROUP_DIM, POOL_GROUP_DIM), f32) * (POOL_GROUP_DIM ** -0.5),
        "pool_scale": 1.0 + 0.02 * nrm(ks[6], (DEPTH, POOL_WIDTH), f32),
        "attn_q_gain": 1.0 + 0.02 * nrm(ks[7], (DEPTH, ATT_HEAD_DIM), f32),
        "attn_k_gain": 1.0 + 0.02 * nrm(ks[8], (DEPTH, ATT_HEAD_DIM), f32),
        "attn_sink": 0.5 * nrm(ks[9], (DEPTH, ATT_Q_HEADS), f32),
        "w_ret": nrm(ks[10], (DEPTH, RET_WIDTH, D), f32) * (RET_WIDTH ** -0.5),
        "w_pool": nrm(ks[11], (DEPTH, POOL_WIDTH, D), f32) * (POOL_WIDTH ** -0.5),
        "w_att": nrm(ks[12], (DEPTH, ATT_WIDTH, D), f32) * (ATT_WIDTH ** -0.5),
        "w_out": nrm(ks[13], (DEPTH, D, D), f32) * (D ** -0.5),
    }


def reference(x, norm_g, w_in, ret_decay_fwd, ret_decay_bwd, pool_w, pool_scale,
              attn_q_gain, attn_k_gain, attn_sink, w_ret, w_pool, w_att, w_out):
    for l in range(DEPTH):
        x = hybrid_layer(x, norm_g[l], w_in[l], ret_decay_fwd[l], ret_decay_bwd[l],
                         pool_w[l], pool_scale[l], attn_q_gain[l], attn_k_gain[l],
                         attn_sink[l], w_ret[l], w_pool[l], w_att[l], w_out[l])
    return x
```

```python
import functools

import jax
import jax.numpy as jnp
from jax import lax
from jax.experimental import pallas as pl
from jax.experimental.pallas import tpu as pltpu

F32 = jnp.float32
BF16 = jnp.bfloat16

D_MODEL = 2048
BRANCH = D_MODEL // 2
RET_HEADS = 4
RET_DH = BRANCH // RET_HEADS
RET_CHUNK = 128
RET_ROPE_BASE = 10000.0
POOL_WINDOWS = (2, 4, 8, 16)
POOL_DG = BRANCH // len(POOL_WINDOWS)
POOL_HALO = 16
ATT_DH = 128
ATT_Q_HEADS = BRANCH // ATT_DH
ATT_KV_HEADS = ATT_Q_HEADS // 4
ATT_GROUP = ATT_Q_HEADS // ATT_KV_HEADS
ATT_BLOCK = 128
ATT_WINDOW = 128
ROPE_THETA = 500000.0
ROPE_DIMS = ATT_DH // 4
RMS_EPS = 1e-6
NEG_BIG = -1e30

N_GATES = 3
MG_WIDTH = N_GATES * D_MODEL
IN_WIDTH = MG_WIDTH + 4 * BRANCH + 2 * BRANCH + 2 * BRANCH + 2 * ATT_KV_HEADS * ATT_DH

OFF_MG = 0
OFF_RQ = MG_WIDTH
OFF_RK = OFF_RQ + BRANCH
OFF_RV = OFF_RK + BRANCH
OFF_RG = OFF_RV + BRANCH
OFF_PV = OFF_RG + BRANCH
OFF_PG = OFF_PV + BRANCH
OFF_AQ = OFF_PG + BRANCH
OFF_AK = OFF_AQ + BRANCH
OFF_AV = OFF_AK + ATT_KV_HEADS * ATT_DH
OFF_AG = OFF_AV + ATT_KV_HEADS * ATT_DH
assert OFF_AG + BRANCH == IN_WIDTH

VMEM_LIMIT = 56 * 1024 * 1024


def _sigmoid(x):
    return jax.nn.sigmoid(x)


def _silu(x):
    return x * jax.nn.sigmoid(x)


def _inproj_kernel(x_ref, g_ref, w_ref, z_ref, h_ref):
    @pl.when(pl.program_id(1) == 0)
    def _():
        x = x_ref[...]
        ms = jnp.mean(x * x, axis=-1, keepdims=True)
        h_ref[...] = (x * lax.rsqrt(ms + RMS_EPS) * g_ref[...]).astype(BF16)

    z_ref[...] = jnp.dot(h_ref[...], w_ref[...],
                         preferred_element_type=F32).astype(z_ref.dtype)


def _inproj(x2, g, w, tm=1024, tn=512):
    m, d = x2.shape
    n = w.shape[1]
    return pl.pallas_call(
        _inproj_kernel,
        grid=(m // tm, n // tn),
        in_specs=[
            pl.BlockSpec((tm, d), lambda i, j: (i, 0)),
            pl.BlockSpec((1, d), lambda i, j: (0, 0)),
            pl.BlockSpec((d, tn), lambda i, j: (0, j)),
        ],
        out_specs=pl.BlockSpec((tm, tn), lambda i, j: (i, j)),
        out_shape=jax.ShapeDtypeStruct((m, n), BF16),
        scratch_shapes=[pltpu.VMEM((tm, d), BF16)],
        compiler_params=pltpu.CompilerParams(
            dimension_semantics=("arbitrary", "arbitrary"),
            vmem_limit_bytes=VMEM_LIMIT),
        name="inproj",
    )(x2, g, w)


def _ret_kernel(af_ref, ab_ref, q_ref, k_ref, v_ref, g_ref, cos_ref, sin_ref,
                o_ref, qb_s, kb_s, acc_s, sf_s, sb_s, *, seq):
    C = RET_CHUNK
    n_chunks = seq // C
    half = RET_DH // 2
    h = pl.program_id(1)
    lg_f = -jnp.exp(jnp.full((C, C), af_ref[h], F32))
    lg_b = -jnp.exp(jnp.full((C, C), ab_ref[h], F32))
    row = lax.broadcasted_iota(jnp.int32, (C, C), 0).astype(F32)
    col = lax.broadcasted_iota(jnp.int32, (C, C), 1).astype(F32)
    lag = row - col
    alag = jnp.abs(lag)
    dmask = jnp.where(lag >= 0, jnp.exp(lg_f * alag), jnp.exp(lg_b * alag))
    w_f = jnp.exp(lg_f * (C - 1.0 - row))
    w_b = jnp.exp(lg_b * row)
    q_f = jnp.exp(lg_f * (row + 1.0))
    q_b = jnp.exp(lg_b * (C - row))
    dec_f = jnp.exp(lg_f * C)[:1, :]
    dec_b = jnp.exp(lg_b * C)[:1, :]
    dec_f = jnp.concatenate([dec_f, dec_f], axis=1)
    dec_b = jnp.concatenate([dec_b, dec_b], axis=1)

    def both(t):
        return jnp.concatenate([t, t], axis=1)

    sf_s[...] = jnp.zeros_like(sf_s)
    sb_s[...] = jnp.zeros_like(sb_s)

    def rot(x, cos, sin):
        x1, x2 = x[:, :half], x[:, half:]
        return jnp.concatenate([x1 * cos - x2 * sin, x2 * cos + x1 * sin], axis=1)

    def fwd(n, carry):
        r = pl.ds(pl.multiple_of(n * C, C), C)
        cos = cos_ref[r, :]
        sin = sin_ref[r, :]
        qr = rot(q_ref[r, :].astype(F32), cos, sin)
        kr = rot(k_ref[r, :].astype(F32), cos, sin) * (RET_DH ** -0.5)
        v = v_ref[r, :]
        qr16 = qr.astype(BF16)
        kr16 = kr.astype(BF16)
        qb_s[r, :] = (qr * both(q_b)).astype(BF16)
        kb_s[r, :] = (kr * both(w_b)).astype(BF16)
        scores = lax.dot_general(qr16, kr16, (((1,), (1,)), ((), ())),
                                 preferred_element_type=F32) * dmask
        o = jnp.dot(scores.astype(BF16), v, preferred_element_type=F32)
        o = o + jnp.dot((qr * both(q_f)).astype(BF16), sf_s[...].astype(BF16),
                        preferred_element_type=F32)
        acc_s[r, :] = o
        kv = lax.dot_general((kr * both(w_f)).astype(BF16), v,
                             (((0,), (0,)), ((), ())), preferred_element_type=F32)
        sf_s[...] = sf_s[...] * dec_f + kv
        return carry

    lax.fori_loop(0, n_chunks, fwd, 0)

    def bwd(i, carry):
        n = n_chunks - 1 - i
        r = pl.ds(pl.multiple_of(n * C, C), C)
        o = acc_s[r, :] + jnp.dot(qb_s[r, :], sb_s[...].astype(BF16),
                                  preferred_element_type=F32)
        y = o * lax.rsqrt(jnp.mean(o * o, axis=-1, keepdims=True) + RMS_EPS)
        o_ref[r, :] = (y * _silu(g_ref[r, :].astype(F32))).astype(o_ref.dtype)
        kv = lax.dot_general(kb_s[r, :], v_ref[r, :], (((0,), (0,)), ((), ())),
                             preferred_element_type=F32)
        sb_s[...] = sb_s[...] * dec_b + kv
        return carry

    lax.fori_loop(0, n_chunks, bwd, 0)


def _retention(z, a_f, a_b, cos, sin, batch, seq):
    dh = RET_DH
    blk = lambda off: pl.BlockSpec((seq, dh), lambda b, h, o=off // dh: (b, o + h))
    smem = pl.BlockSpec(memory_space=pltpu.SMEM)
    tab = pl.BlockSpec((seq, dh // 2), lambda b, h: (0, 0))
    return pl.pallas_call(
        functools.partial(_ret_kernel, seq=seq),
        grid=(batch, RET_HEADS),
        in_specs=[smem, smem, blk(OFF_RQ), blk(OFF_RK), blk(OFF_RV), blk(OFF_RG), tab, tab],
        out_specs=pl.BlockSpec((seq, dh), lambda b, h: (b, h)),
        out_shape=jax.ShapeDtypeStruct((batch * seq, BRANCH), BF16),
        scratch_shapes=[
            pltpu.VMEM((seq, dh), BF16),
            pltpu.VMEM((seq, dh), BF16),
            pltpu.VMEM((seq, dh), F32),
            pltpu.VMEM((dh, dh), F32),
            pltpu.VMEM((dh, dh), F32),
        ],
        compiler_params=pltpu.CompilerParams(
            dimension_semantics=("arbitrary", "arbitrary"),
            vmem_limit_bytes=VMEM_LIMIT),
        name="retention",
    )(a_f, a_b, z, z, z, z, cos, sin)


def _pool_kernel(u_ref, g_ref, w_ref, sc_ref, o_ref, pad_s, *, seq, tile):
    H = POOL_HALO
    grp = pl.program_id(1)
    zeros = jnp.zeros((H, POOL_DG), F32)
    pad_s[0:H, :] = zeros
    pad_s[H + seq:H + seq + H, :] = zeros

    def fill(t, carry):
        r = pl.multiple_of(t * tile, tile)
        pad_s[pl.ds(H + r, tile), :] = u_ref[pl.ds(r, tile), :].astype(F32)
        return carry

    lax.fori_loop(0, seq // tile, fill, 0)

    w = w_ref[0]
    scale = sc_ref[...]

    for gi, win in enumerate(POOL_WINDOWS):
        @pl.when(grp == gi)
        def _(win=win):
            hw = win // 2

            def body(t, carry):
                r = pl.multiple_of(t * tile, tile)
                n_ext = tile + 2 * H
                ext = pad_s[pl.ds(r, n_ext), :]
                s = ext
                step = 1
                while step < win:
                    s = s + pltpu.roll(s, n_ext - step, 0)
                    step *= 2
                acc = pltpu.roll(s, hw, 0)[H:H + tile]
                pos = r + lax.broadcasted_iota(jnp.int32, (tile, POOL_DG), 0)
                lo = jnp.maximum(pos - hw, 0)
                hi = jnp.minimum(pos + hw, seq)
                cnt = (hi - lo).astype(F32)
                p = acc / cnt - ext[H:H + tile]
                y = jnp.dot(p.astype(BF16), w, preferred_element_type=F32) * scale
                gate = _silu(g_ref[pl.ds(r, tile), :].astype(F32))
                o_ref[pl.ds(r, tile), :] = (y * gate).astype(o_ref.dtype)
                return carry

            lax.fori_loop(0, seq // tile, body, 0)


def _pool(z, pool_w16, pool_scale, batch, seq, tile=256):
    dg = POOL_DG
    n_groups = len(POOL_WINDOWS)
    return pl.pallas_call(
        functools.partial(_pool_kernel, seq=seq, tile=tile),
        grid=(batch, n_groups),
        in_specs=[
            pl.BlockSpec((seq, dg), lambda b, g: (b, OFF_PV // dg + g)),
            pl.BlockSpec((seq, dg), lambda b, g: (b, OFF_PG // dg + g)),
            pl.BlockSpec((1, dg, dg), lambda b, g: (g, 0, 0)),
            pl.BlockSpec((1, dg), lambda b, g: (0, g)),
        ],
        out_specs=pl.BlockSpec((seq, dg), lambda b, g: (b, g)),
        out_shape=jax.ShapeDtypeStruct((batch * seq, BRANCH), BF16),
        scratch_shapes=[pltpu.VMEM((seq + 2 * POOL_HALO, dg), F32)],
        compiler_params=pltpu.CompilerParams(
            dimension_semantics=("arbitrary", "arbitrary"),
            vmem_limit_bytes=VMEM_LIMIT),
        name="pool",
    )(z, z, pool_w16, pool_scale)


def _rope_partial(x, cos, sin_signed):
    lane = lax.broadcasted_iota(jnp.int32, x.shape, 1)
    up = pltpu.roll(x, ATT_DH - ROPE_DIMS // 2, 1)
    down = pltpu.roll(x, ROPE_DIMS // 2, 1)
    partner = jnp.where(lane < ROPE_DIMS // 2, up, down)
    return x * cos + partner * sin_signed


def _rms_rows(x, gain):
    return x * lax.rsqrt(jnp.mean(x * x, axis=-1, keepdims=True) + RMS_EPS) * gain


def _att_kernel(sink_ref, q_ref, k_ref, v_ref, g_ref, qg_ref, kg_ref, cos_ref, sin_ref,
                o_ref, kn_s, *, seq):
    L = ATT_BLOCK
    G = ATT_GROUP
    kk = pl.program_id(1)
    n = pl.program_id(2)
    n_blocks = seq // L

    @pl.when(n == 0)
    def _():
        def norm_k(t, carry):
            r = pl.ds(pl.multiple_of(t * L, L), L)
            kn = _rms_rows(k_ref[r, :].astype(F32), kg_ref[...])
            kn_s[r, :] = _rope_partial(kn, cos_ref[r, :], sin_ref[r, :]).astype(BF16)
            return carry
        lax.fori_loop(0, n_blocks, norm_k, 0)

    qrows = pl.ds(pl.multiple_of(n * L, L), L)
    cos = cos_ref[qrows, :]
    sin = sin_ref[qrows, :]
    qs = []
    for h in range(G):
        qh = _rms_rows(q_ref[:, h * ATT_DH:(h + 1) * ATT_DH].astype(F32), qg_ref[...])
        qs.append(_rope_partial(qh, cos, sin).astype(BF16))
    q = jnp.concatenate(qs, axis=0)

    start = jnp.clip((n - 1) * L, 0, seq - 3 * L)
    start = pl.multiple_of(start, L)
    kw = kn_s[pl.ds(start, 3 * L), :]
    vw = v_ref[pl.ds(start, 3 * L), :]
    s = lax.dot_general(q, kw, (((1,), (1,)), ((), ())),
                        preferred_element_type=F32) * (ATT_DH ** -0.5)
    row = lax.broadcasted_iota(jnp.int32, (L, 3 * L), 0)
    col = lax.broadcasted_iota(jnp.int32, (L, 3 * L), 1)
    diff = col - row + (start - n * L)
    valid = jnp.abs(diff) <= ATT_WINDOW
    outs = []
    for h in range(G):
        sh = jnp.where(valid, s[h * L:(h + 1) * L, :], NEG_BIG)
        sk = jnp.full((L, 1), sink_ref[kk * G + h], F32)
        m = jnp.maximum(jnp.max(sh, axis=-1, keepdims=True), sk)
        p = jnp.exp(sh - m)
        denom = jnp.sum(p, axis=-1, keepdims=True) + jnp.exp(sk - m)
        p = p / denom
        oh = jnp.dot(p.astype(BF16), vw, preferred_element_type=F32)
        gate = _silu(g_ref[:, h * ATT_DH:(h + 1) * ATT_DH].astype(F32))
        outs.append((oh * gate).astype(o_ref.dtype))
    o_ref[...] = jnp.concatenate(outs, axis=1)


def _attention(z, sink, q_gain, k_gain, cos, sin, batch, seq):
    L = ATT_BLOCK
    n_blocks = seq // L
    gw = ATT_GROUP * ATT_DH
    smem = pl.BlockSpec(memory_space=pltpu.SMEM)
    tab = pl.BlockSpec((seq, ATT_DH), lambda b, k, n: (0, 0))
    gain = pl.BlockSpec((1, ATT_DH), lambda b, k, n: (0, 0))
    return pl.pallas_call(
        functools.partial(_att_kernel, seq=seq),
        grid=(batch, ATT_KV_HEADS, n_blocks),
        in_specs=[
            smem,
            pl.BlockSpec((L, gw), lambda b, k, n: (b * n_blocks + n, OFF_AQ // gw + k)),
            pl.BlockSpec((seq, ATT_DH), lambda b, k, n: (b, OFF_AK // ATT_DH + k)),
            pl.BlockSpec((seq, ATT_DH), lambda b, k, n: (b, OFF_AV // ATT_DH + k)),
            pl.BlockSpec((L, gw), lambda b, k, n: (b * n_blocks + n, OFF_AG // gw + k)),
            gain, gain, tab, tab,
        ],
        out_specs=pl.BlockSpec((L, gw), lambda b, k, n: (b * n_blocks + n, k)),
        out_shape=jax.ShapeDtypeStruct((batch * seq, BRANCH), BF16),
        scratch_shapes=[pltpu.VMEM((seq, ATT_DH), BF16)],
        compiler_params=pltpu.CompilerParams(
            dimension_semantics=("arbitrary", "arbitrary", "arbitrary"),
            vmem_limit_bytes=VMEM_LIMIT),
        name="attention",
    )(sink, z, z, z, z, q_gain, k_gain, cos, sin)


def _merge_kernel(x_ref, ya_ref, yb_ref, yc_ref, g0_ref, g1_ref, g2_ref,
                  wa_ref, wb_ref, wc_ref, wo_ref, o_ref, m_s, *, tc):
    ya = ya_ref[...]
    yb = yb_ref[...]
    yc = yc_ref[...]
    for c in range(D_MODEL // tc):
        cs = slice(c * tc, (c + 1) * tc)
        a = jnp.dot(ya, wa_ref[:, cs], preferred_element_type=F32)
        b = jnp.dot(yb, wb_ref[:, cs], preferred_element_type=F32)
        cc = jnp.dot(yc, wc_ref[:, cs], preferred_element_type=F32)
        merged = (_sigmoid(g0_ref[:, cs].astype(F32)) * a
                  + _sigmoid(g1_ref[:, cs].astype(F32)) * b
                  + _sigmoid(g2_ref[:, cs].astype(F32)) * cc)
        m_s[:, cs] = merged.astype(BF16)
    o_ref[...] = x_ref[...] + jnp.dot(m_s[...], wo_ref[...], preferred_element_type=F32)


def _merge(x2, ya, yb, yc, z, wa, wb, wc, wo, tm=256, tc=512):
    m, d = x2.shape
    row = lambda w: pl.BlockSpec((tm, w), lambda i: (i, 0))
    gate = lambda j: pl.BlockSpec((tm, d), lambda i, j=j: (i, OFF_MG // d + j))
    wspec = lambda k: pl.BlockSpec((k, d), lambda i: (0, 0), pipeline_mode=pl.Buffered(1))
    return pl.pallas_call(
        functools.partial(_merge_kernel, tc=tc),
        grid=(m // tm,),
        in_specs=[row(d), row(BRANCH), row(BRANCH), row(BRANCH),
                  gate(0), gate(1), gate(2),
                  wspec(BRANCH), wspec(BRANCH), wspec(BRANCH), wspec(d)],
        out_specs=row(d),
        out_shape=jax.ShapeDtypeStruct((m, d), F32),
        scratch_shapes=[pltpu.VMEM((tm, d), BF16)],
        compiler_params=pltpu.CompilerParams(
            dimension_semantics=("arbitrary",),
            vmem_limit_bytes=VMEM_LIMIT),
        name="merge",
    )(x2, ya, yb, yc, z, z, z, wa, wb, wc, wo)


def _ret_tables(seq):
    half = RET_DH // 2
    inv = 1.0 / (RET_ROPE_BASE ** jnp.linspace(0.0, 1.0, half, dtype=F32))
    ang = jnp.arange(seq, dtype=F32)[:, None] * inv[None, :]
    return jnp.cos(ang), jnp.sin(ang)


def _att_tables(seq):
    half = ROPE_DIMS // 2
    inv = ROPE_THETA ** (-jnp.arange(half, dtype=F32) / half)
    ang = jnp.arange(seq, dtype=F32)[:, None] * inv[None, :]
    c, s = jnp.cos(ang), jnp.sin(ang)
    rest = ATT_DH - ROPE_DIMS
    cos = jnp.concatenate([c, c, jnp.ones((seq, rest), F32)], axis=1)
    sin = jnp.concatenate([-s, s, jnp.zeros((seq, rest), F32)], axis=1)
    return cos, sin


def kernel(x, norm_g, w_in, ret_decay_fwd, ret_decay_bwd, pool_w, pool_scale,
           attn_q_gain, attn_k_gain, attn_sink, w_ret, w_pool, w_att, w_out):
    batch, seq, d = x.shape
    depth = w_in.shape[0]
    assert d == D_MODEL and w_in.shape[2] == IN_WIDTH
    assert seq % RET_CHUNK == 0 and seq >= 3 * ATT_BLOCK
    rcos, rsin = _ret_tables(seq)
    acos, asin = _att_tables(seq)
    split = IN_WIDTH - MG_WIDTH
    x2 = x.reshape(batch * seq, d)
    for l in range(depth):
        w_l = jnp.concatenate([w_in[l, :, split:], w_in[l, :, :split]], axis=1).astype(BF16)
        z = _inproj(x2, norm_g[l][None, :], w_l)
        ya = _retention(z, ret_decay_fwd[l], ret_decay_bwd[l], rcos, rsin, batch, seq)
        yb = _pool(z, pool_w[l].astype(BF16), pool_scale[l][None, :], batch, seq)
        yc = _attention(z, attn_sink[l], attn_q_gain[l][None, :], attn_k_gain[l][None, :],
                        acos, asin, batch, seq)
        x2 = _merge(x2, ya, yb, yc, z, w_ret[l].astype(BF16), w_pool[l].astype(BF16),
                    w_att[l].astype(BF16), w_out[l].astype(BF16))
    return x2.reshape(batch, seq, d)
```

```python
import functools

import jax
import jax.numpy as jnp
from jax import lax
from jax.experimental import pallas as pl
from jax.experimental.pallas import tpu as pltpu

F32 = jnp.float32
BF16 = jnp.bfloat16

D_MODEL = 2048
BRANCH = D_MODEL // 2
RET_HEADS = 4
RET_DH = BRANCH // RET_HEADS
RET_CHUNK = 256
RET_ROPE_BASE = 10000.0
POOL_WINDOWS = (2, 4, 8, 16)
POOL_DG = BRANCH // len(POOL_WINDOWS)
POOL_HALO = 16
ATT_DH = 128
ATT_Q_HEADS = BRANCH // ATT_DH
ATT_KV_HEADS = ATT_Q_HEADS // 4
ATT_GROUP = ATT_Q_HEADS // ATT_KV_HEADS
ATT_BLOCK = 128
ATT_WINDOW = 128
ROPE_THETA = 500000.0
ROPE_DIMS = ATT_DH // 4
RMS_EPS = 1e-6
NEG_BIG = -1e30

N_GATES = 3
MG_WIDTH = N_GATES * D_MODEL
GATE_BLOCK = 512

OFF_RQ = 0
OFF_RK = OFF_RQ + BRANCH
OFF_RV = OFF_RK + BRANCH
OFF_RG = OFF_RV + BRANCH
OFF_PV = OFF_RG + BRANCH
OFF_PG = OFF_PV + BRANCH
OFF_AQ = OFF_PG + BRANCH
OFF_AK = OFF_AQ + BRANCH
OFF_AV = OFF_AK + ATT_KV_HEADS * ATT_DH
OFF_AG = OFF_AV + ATT_KV_HEADS * ATT_DH
OFF_MG = OFF_AG + BRANCH
IN_WIDTH = OFF_MG + MG_WIDTH
assert OFF_MG % GATE_BLOCK == 0 and D_MODEL % GATE_BLOCK == 0

VMEM_LIMIT = 56 * 1024 * 1024


def _sigmoid(x):
    return jax.nn.sigmoid(x)


def _silu(x):
    return x * jax.nn.sigmoid(x)


def _inproj_kernel(x_ref, g_ref, w_ref, z_ref, h_ref):
    @pl.when(pl.program_id(1) == 0)
    def _():
        x = x_ref[...]
        ms = jnp.mean(x * x, axis=-1, keepdims=True)
        h_ref[...] = (x * lax.rsqrt(ms + RMS_EPS) * g_ref[...]).astype(BF16)

    z_ref[...] = jnp.dot(h_ref[...], w_ref[...],
                         preferred_element_type=F32).astype(z_ref.dtype)


def _inproj(x2, g, w, tm=1024, tn=512):
    m, d = x2.shape
    n = w.shape[1]
    return pl.pallas_call(
        _inproj_kernel,
        grid=(m // tm, n // tn),
        in_specs=[
            pl.BlockSpec((tm, d), lambda i, j: (i, 0)),
            pl.BlockSpec((1, d), lambda i, j: (0, 0)),
            pl.BlockSpec((d, tn), lambda i, j: (0, j)),
        ],
        out_specs=pl.BlockSpec((tm, tn), lambda i, j: (i, j)),
        out_shape=jax.ShapeDtypeStruct((m, n), BF16),
        scratch_shapes=[pltpu.VMEM((tm, d), BF16)],
        compiler_params=pltpu.CompilerParams(
            dimension_semantics=("arbitrary", "arbitrary"),
            vmem_limit_bytes=VMEM_LIMIT),
        name="inproj",
    )(x2, g, w)


_QF, _QB, _WF, _WB = range(4)


def _ret_kernel(af_ref, ab_ref, q_ref, k_ref, v_ref, g_ref, cos_ref, sin_ref,
                o_ref, qx_s, kx_s, acc_s, sf_s, sb_s, dm_s, tab_s, *, seq):
    C = RET_CHUNK
    n_chunks = seq // C
    half = RET_DH // 2
    h = pl.program_id(1)

    lg_f = -jnp.exp(jnp.full((C, half), af_ref[h], F32))
    lg_b = -jnp.exp(jnp.full((C, half), ab_ref[h], F32))
    pos = lax.broadcasted_iota(jnp.int32, (C, half), 0).astype(F32)
    tab_s[_QF] = jnp.exp(lg_f * (pos + 1.0))
    tab_s[_QB] = jnp.exp(lg_b * (C - pos))
    tab_s[_WF] = jnp.exp(lg_f * (C - 1.0 - pos))
    tab_s[_WB] = jnp.exp(lg_b * pos)
    dec_f = jnp.exp(lg_f[:1] * C)
    dec_b = jnp.exp(lg_b[:1] * C)
    dec_f = jnp.concatenate([dec_f, dec_f], axis=1)
    dec_b = jnp.concatenate([dec_b, dec_b], axis=1)
    for c in range(C // half):
        row = lax.broadcasted_iota(jnp.int32, (C, half), 0)
        col = lax.broadcasted_iota(jnp.int32, (C, half), 1) + c * half
        lag = (row - col).astype(F32)
        alag = jnp.abs(lag)
        dm_s[:, c * half:(c + 1) * half] = jnp.where(
            lag >= 0, jnp.exp(lg_f * alag), jnp.exp(lg_b * alag))
    sf_s[...] = jnp.zeros_like(sf_s)
    sb_s[...] = jnp.zeros_like(sb_s)

    def rot(x, cos, sin):
        x1, x2 = x[:, :half], x[:, half:]
        return x1 * cos - x2 * sin, x2 * cos + x1 * sin

    def scaled16(x12, t):
        tab = tab_s[t]
        return jnp.concatenate([x12[0] * tab, x12[1] * tab], axis=1).astype(BF16)

    def update_state(s_ref, dec, k16, v):
        kv = lax.dot_general(k16, v, (((0,), (0,)), ((), ())), preferred_element_type=F32)
        s_ref[...] = s_ref[...] * dec + kv

    def first_visit(n, s_ref, dec, t_q, t_k, t_q_later, t_k_later):
        r = pl.ds(pl.multiple_of(n * C, C), C)
        cos = cos_ref[r, :]
        sin = sin_ref[r, :]
        q12 = rot(q_ref[r, :].astype(F32), cos, sin)
        k12 = rot(k_ref[r, :].astype(F32), cos, sin)
        k12 = (k12[0] * (RET_DH ** -0.5), k12[1] * (RET_DH ** -0.5))
        v = v_ref[r, :]
        q16 = jnp.concatenate(q12, axis=1).astype(BF16)
        k16 = jnp.concatenate(k12, axis=1).astype(BF16)
        scores = lax.dot_general(q16, k16, (((1,), (1,)), ((), ())),
                                 preferred_element_type=F32) * dm_s[...]
        o = jnp.dot(scores.astype(BF16), v, preferred_element_type=F32)
        o = o + jnp.dot(scaled16(q12, t_q), s_ref[...].astype(BF16),
                        preferred_element_type=F32)
        acc_s[r, :] = o
        update_state(s_ref, dec, scaled16(k12, t_k), v)
        qx_s[r, :] = scaled16(q12, t_q_later)
        kx_s[r, :] = scaled16(k12, t_k_later)

    def second_visit(n, s_ref, dec):
        r = pl.ds(pl.multiple_of(n * C, C), C)
        o = acc_s[r, :] + jnp.dot(qx_s[r, :], s_ref[...].astype(BF16),
                                  preferred_element_type=F32)
        y = o * lax.rsqrt(jnp.mean(o * o, axis=-1, keepdims=True) + RMS_EPS)
        o_ref[r, :] = (y * _silu(g_ref[r, :].astype(F32))).astype(o_ref.dtype)
        update_state(s_ref, dec, kx_s[r, :], v_ref[r, :])

    def first_half(i, carry):
        first_visit(i, sf_s, dec_f, _QF, _WF, _QB, _WB)
        first_visit(n_chunks - 1 - i, sb_s, dec_b, _QB, _WB, _QF, _WF)
        return carry

    def second_half(i, carry):
        second_visit(i, sf_s, dec_f)
        second_visit(n_chunks - 1 - i, sb_s, dec_b)
        return carry

    lax.fori_loop(0, n_chunks // 2, first_half, 0)
    lax.fori_loop(n_chunks // 2, n_chunks, second_half, 0)


def _retention(z, a_f, a_b, cos, sin, batch, seq):
    dh = RET_DH
    assert seq % (2 * RET_CHUNK) == 0
    blk = lambda off: pl.BlockSpec((seq, dh), lambda b, h, o=off // dh: (b, o + h))
    smem = pl.BlockSpec(memory_space=pltpu.SMEM)
    tab = pl.BlockSpec((seq, dh // 2), lambda b, h: (0, 0), pipeline_mode=pl.Buffered(1))
    return pl.pallas_call(
        functools.partial(_ret_kernel, seq=seq),
        grid=(batch, RET_HEADS),
        in_specs=[smem, smem, blk(OFF_RQ), blk(OFF_RK), blk(OFF_RV), blk(OFF_RG), tab, tab],
        out_specs=pl.BlockSpec((seq, dh), lambda b, h: (b, h)),
        out_shape=jax.ShapeDtypeStruct((batch * seq, BRANCH), BF16),
        scratch_shapes=[
            pltpu.VMEM((seq, dh), BF16),
            pltpu.VMEM((seq, dh), BF16),
            pltpu.VMEM((seq, dh), F32),
            pltpu.VMEM((dh, dh), F32),
            pltpu.VMEM((dh, dh), F32),
            pltpu.VMEM((RET_CHUNK, RET_CHUNK), F32),
            pltpu.VMEM((4, RET_CHUNK, dh // 2), F32),
        ],
        compiler_params=pltpu.CompilerParams(
            dimension_semantics=("arbitrary", "arbitrary"),
            vmem_limit_bytes=VMEM_LIMIT),
        name="retention",
    )(a_f, a_b, z, z, z, z, cos, sin)


def _pool_kernel(u_ref, g_ref, w_ref, sc_ref, o_ref, pad_s, *, seq, tile):
    H = POOL_HALO
    grp = pl.program_id(1)
    zeros = jnp.zeros((H, POOL_DG), F32)
    pad_s[0:H, :] = zeros
    pad_s[H + seq:H + seq + H, :] = zeros

    def fill(t, carry):
        r = pl.multiple_of(t * tile, tile)
        pad_s[pl.ds(H + r, tile), :] = u_ref[pl.ds(r, tile), :].astype(F32)
        return carry

    lax.fori_loop(0, seq // tile, fill, 0)

    w = w_ref[0]
    scale = sc_ref[...]

    for gi, win in enumerate(POOL_WINDOWS):
        @pl.when(grp == gi)
        def _(win=win):
            hw = win // 2

            def body(t, carry):
                r = pl.multiple_of(t * tile, tile)
                n_ext = tile + 2 * H
                ext = pad_s[pl.ds(r, n_ext), :]
                s = ext
                step = 1
                while step < win:
                    s = s + pltpu.roll(s, n_ext - step, 0)
                    step *= 2
                acc = pltpu.roll(s, hw, 0)[H:H + tile]
                pos = r + lax.broadcasted_iota(jnp.int32, (tile, POOL_DG), 0)
                lo = jnp.maximum(pos - hw, 0)
                hi = jnp.minimum(pos + hw, seq)
                cnt = (hi - lo).astype(F32)
                p = acc / cnt - ext[H:H + tile]
                y = jnp.dot(p.astype(BF16), w, preferred_element_type=F32) * scale
                gate = _silu(g_ref[pl.ds(r, tile), :].astype(F32))
                o_ref[pl.ds(r, tile), :] = (y * gate).astype(o_ref.dtype)
                return carry

            lax.fori_loop(0, seq // tile, body, 0)


def _pool(z, pool_w16, pool_scale, batch, seq, tile=256):
    dg = POOL_DG
    n_groups = len(POOL_WINDOWS)
    return pl.pallas_call(
        functools.partial(_pool_kernel, seq=seq, tile=tile),
        grid=(batch, n_groups),
        in_specs=[
            pl.BlockSpec((seq, dg), lambda b, g: (b, OFF_PV // dg + g)),
            pl.BlockSpec((seq, dg), lambda b, g: (b, OFF_PG // dg + g)),
            pl.BlockSpec((1, dg, dg), lambda b, g: (g, 0, 0)),
            pl.BlockSpec((1, dg), lambda b, g: (0, g)),
        ],
        out_specs=pl.BlockSpec((seq, dg), lambda b, g: (b, g)),
        out_shape=jax.ShapeDtypeStruct((batch * seq, BRANCH), BF16),
        scratch_shapes=[pltpu.VMEM((seq + 2 * POOL_HALO, dg), F32)],
        compiler_params=pltpu.CompilerParams(
            dimension_semantics=("arbitrary", "arbitrary"),
            vmem_limit_bytes=VMEM_LIMIT),
        name="pool",
    )(z, z, pool_w16, pool_scale)


def _rope_partial(x, cos, sin_signed):
    lane = lax.broadcasted_iota(jnp.int32, x.shape, 1)
    up = pltpu.roll(x, ATT_DH - ROPE_DIMS // 2, 1)
    down = pltpu.roll(x, ROPE_DIMS // 2, 1)
    partner = jnp.where(lane < ROPE_DIMS // 2, up, down)
    return x * cos + partner * sin_signed


def _rms_rows(x, gain):
    return x * lax.rsqrt(jnp.mean(x * x, axis=-1, keepdims=True) + RMS_EPS) * gain


def _att_kernel(sink_ref, q_ref, k_ref, v_ref, g_ref, qg_ref, kg_ref, cos_ref, sin_ref,
                o_ref, kn_s, *, seq, nq):
    L = ATT_BLOCK
    G = ATT_GROUP
    kk = pl.program_id(1)
    n = pl.program_id(2)
    n_blocks = seq // L

    @pl.when(n == 0)
    def _():
        def norm_k(t, carry):
            r = pl.ds(pl.multiple_of(t * L, L), L)
            kn = _rms_rows(k_ref[r, :].astype(F32), kg_ref[...])
            kn_s[r, :] = _rope_partial(kn, cos_ref[r, :], sin_ref[r, :]).astype(BF16)
            return carry
        lax.fori_loop(0, n_blocks, norm_k, 0)

    scale = ATT_DH ** -0.5
    exp2_coef = scale * 1.4426950408889634
    row = lax.broadcasted_iota(jnp.int32, (L, 3 * L), 0)
    col = lax.broadcasted_iota(jnp.int32, (L, 3 * L), 1)
    for blk in range(nq):
        nb = n * nq + blk
        qrows = pl.ds(pl.multiple_of(nb * L, L), L)
        cos = cos_ref[qrows, :]
        sin = sin_ref[qrows, :]
        rows = slice(blk * L, (blk + 1) * L)
        qs = []
        for h in range(G):
            qh = _rms_rows(q_ref[rows, h * ATT_DH:(h + 1) * ATT_DH].astype(F32), qg_ref[...])
            qs.append(_rope_partial(qh, cos, sin).astype(BF16))
        q = jnp.concatenate(qs, axis=0)

        start = jnp.clip((nb - 1) * L, 0, seq - 3 * L)
        start = pl.multiple_of(start, L)
        kw = kn_s[pl.ds(start, 3 * L), :]
        vw = v_ref[pl.ds(start, 3 * L), :]
        s = lax.dot_general(q, kw, (((1,), (1,)), ((), ())),
                            preferred_element_type=F32)
        valid = jnp.abs(col - row + (start - nb * L)) <= ATT_WINDOW
        outs = []
        for h in range(G):
            sh = jnp.where(valid, s[h * L:(h + 1) * L, :], NEG_BIG)
            sk = jnp.full((L, 1), sink_ref[kk * G + h] * (1.0 / scale), F32)
            m = jnp.maximum(jnp.max(sh, axis=-1, keepdims=True), sk)
            p = jnp.exp2((sh - m) * exp2_coef)
            denom = jnp.sum(p, axis=-1, keepdims=True) + jnp.exp2((sk - m) * exp2_coef)
            oh = jnp.dot(p.astype(BF16), vw, preferred_element_type=F32) / denom
            gate = _silu(g_ref[rows, h * ATT_DH:(h + 1) * ATT_DH].astype(F32))
            outs.append((oh * gate).astype(o_ref.dtype))
        o_ref[rows, :] = jnp.concatenate(outs, axis=1)


def _attention(z, sink, q_gain, k_gain, cos, sin, batch, seq, nq=2):
    L = ATT_BLOCK
    n_blocks = seq // L
    assert n_blocks % nq == 0
    n_steps = n_blocks // nq
    gw = ATT_GROUP * ATT_DH
    smem = pl.BlockSpec(memory_space=pltpu.SMEM)
    tab = pl.BlockSpec((seq, ATT_DH), lambda b, k, n: (0, 0))
    gain = pl.BlockSpec((1, ATT_DH), lambda b, k, n: (0, 0))
    return pl.pallas_call(
        functools.partial(_att_kernel, seq=seq, nq=nq),
        grid=(batch, ATT_KV_HEADS, n_steps),
        in_specs=[
            smem,
            pl.BlockSpec((nq * L, gw), lambda b, k, n: (b * n_steps + n, OFF_AQ // gw + k)),
            pl.BlockSpec((seq, ATT_DH), lambda b, k, n: (b, OFF_AK // ATT_DH + k)),
            pl.BlockSpec((seq, ATT_DH), lambda b, k, n: (b, OFF_AV // ATT_DH + k)),
            pl.BlockSpec((nq * L, gw), lambda b, k, n: (b * n_steps + n, OFF_AG // gw + k)),
            gain, gain, tab, tab,
        ],
        out_specs=pl.BlockSpec((nq * L, gw), lambda b, k, n: (b * n_steps + n, k)),
        out_shape=jax.ShapeDtypeStruct((batch * seq, BRANCH), BF16),
        scratch_shapes=[pltpu.VMEM((seq, ATT_DH), BF16)],
        compiler_params=pltpu.CompilerParams(
            dimension_semantics=("arbitrary", "arbitrary", "arbitrary"),
            vmem_limit_bytes=VMEM_LIMIT),
        name="attention",
    )(sink, z, z, z, z, q_gain, k_gain, cos, sin)


def _merge_kernel(x_ref, ya_ref, yb_ref, yc_ref, *rest, n_col):
    gate_refs = rest[:N_GATES * n_col]
    wa_ref, wb_ref, wc_ref, wo_ref, o_ref, m_s = rest[N_GATES * n_col:]
    tc = GATE_BLOCK
    ya = ya_ref[...]
    yb = yb_ref[...]
    yc = yc_ref[...]
    for c in range(n_col):
        cs = slice(c * tc, (c + 1) * tc)
        a = jnp.dot(ya, wa_ref[:, cs], preferred_element_type=F32)
        b = jnp.dot(yb, wb_ref[:, cs], preferred_element_type=F32)
        cc = jnp.dot(yc, wc_ref[:, cs], preferred_element_type=F32)
        g0, g1, g2 = (gate_refs[j * n_col + c][...].astype(F32) for j in range(N_GATES))
        merged = _sigmoid(g0) * a + _sigmoid(g1) * b + _sigmoid(g2) * cc
        m_s[:, cs] = merged.astype(BF16)
    o_ref[...] = x_ref[...] + jnp.dot(m_s[...], wo_ref[...], preferred_element_type=F32)


def _merge(x2, ya, yb, yc, z, wa, wb, wc, wo, tm=256):
    m, d = x2.shape
    n_col = d // GATE_BLOCK
    row = lambda w: pl.BlockSpec((tm, w), lambda i: (i, 0))
    gate = lambda j, c: pl.BlockSpec(
        (tm, GATE_BLOCK), lambda i, o=OFF_MG // GATE_BLOCK + j * n_col + c: (i, o))
    gates = [gate(j, c) for j in range(N_GATES) for c in range(n_col)]
    wspec = lambda k: pl.BlockSpec((k, d), lambda i: (0, 0), pipeline_mode=pl.Buffered(1))
    return pl.pallas_call(
        functools.partial(_merge_kernel, n_col=n_col),
        grid=(m // tm,),
        in_specs=[row(d), row(BRANCH), row(BRANCH), row(BRANCH)] + gates
                 + [wspec(BRANCH), wspec(BRANCH), wspec(BRANCH), wspec(d)],
        out_specs=row(d),
        out_shape=jax.ShapeDtypeStruct((m, d), F32),
        scratch_shapes=[pltpu.VMEM((tm, d), BF16)],
        compiler_params=pltpu.CompilerParams(
            dimension_semantics=("arbitrary",),
            vmem_limit_bytes=VMEM_LIMIT),
        name="merge",
    )(x2, ya, yb, yc, *([z] * len(gates)), wa, wb, wc, wo)


def _ret_tables(seq):
    half = RET_DH // 2
    inv = 1.0 / (RET_ROPE_BASE ** jnp.linspace(0.0, 1.0, half, dtype=F32))
    ang = jnp.arange(seq, dtype=F32)[:, None] * inv[None, :]
    return jnp.cos(ang), jnp.sin(ang)


def _att_tables(seq):
    half = ROPE_DIMS // 2
    inv = ROPE_THETA ** (-jnp.arange(half, dtype=F32) / half)
    ang = jnp.arange(seq, dtype=F32)[:, None] * inv[None, :]
    c, s = jnp.cos(ang), jnp.sin(ang)
    rest = ATT_DH - ROPE_DIMS
    cos = jnp.concatenate([c, c, jnp.ones((seq, rest), F32)], axis=1)
    sin = jnp.concatenate([-s, s, jnp.zeros((seq, rest), F32)], axis=1)
    return cos, sin


def kernel(x, norm_g, w_in, ret_decay_fwd, ret_decay_bwd, pool_w, pool_scale,
           attn_q_gain, attn_k_gain, attn_sink, w_ret, w_pool, w_att, w_out):
    batch, seq, d = x.shape
    depth = w_in.shape[0]
    assert d == D_MODEL and w_in.shape[2] == IN_WIDTH
    assert seq >= 3 * ATT_BLOCK
    rcos, rsin = _ret_tables(seq)
    acos, asin = _att_tables(seq)
    w_in16 = w_in.astype(BF16)
    x2 = x.reshape(batch * seq, d)
    for l in range(depth):
        z = _inproj(x2, norm_g[l][None, :], w_in16[l])
        ya = _retention(z, ret_decay_fwd[l], ret_decay_bwd[l], rcos, rsin, batch, seq)
        yb = _pool(z, pool_w[l].astype(BF16), pool_scale[l][None, :], batch, seq)
        yc = _attention(z, attn_sink[l], attn_q_gain[l][None, :], attn_k_gain[l][None, :],
                        acos, asin, batch, seq)
        x2 = _merge(x2, ya, yb, yc, z, w_ret[l].astype(BF16), w_pool[l].astype(BF16),
                    w_att[l].astype(BF16), w_out[l].astype(BF16))
    return x2.reshape(batch, seq, d)
```

```python
import functools

import jax
import jax.numpy as jnp
from jax import lax
from jax.experimental import pallas as pl
from jax.experimental.pallas import tpu as pltpu

F32 = jnp.float32
BF16 = jnp.bfloat16

D_MODEL = 2048
BRANCH = D_MODEL // 2
RET_HEADS = 4
RET_DH = BRANCH // RET_HEADS
RET_CHUNK = 256
RET_ROPE_BASE = 10000.0
POOL_WINDOWS = (2, 4, 8, 16)
POOL_DG = BRANCH // len(POOL_WINDOWS)
POOL_HALO = 16
ATT_DH = 128
ATT_Q_HEADS = BRANCH // ATT_DH
ATT_KV_HEADS = ATT_Q_HEADS // 4
ATT_GROUP = ATT_Q_HEADS // ATT_KV_HEADS
ATT_BLOCK = 128
ATT_WINDOW = 128
ROPE_THETA = 500000.0
ROPE_DIMS = ATT_DH // 4
RMS_EPS = 1e-6
NEG_BIG = -1e30

N_GATES = 3
MG_WIDTH = N_GATES * D_MODEL
GATE_BLOCK = 512

OFF_RQ = 0
OFF_RK = OFF_RQ + BRANCH
OFF_RV = OFF_RK + BRANCH
OFF_RG = OFF_RV + BRANCH
OFF_PV = OFF_RG + BRANCH
OFF_PG = OFF_PV + BRANCH
OFF_AQ = OFF_PG + BRANCH
OFF_AK = OFF_AQ + BRANCH
OFF_AV = OFF_AK + ATT_KV_HEADS * ATT_DH
OFF_AG = OFF_AV + ATT_KV_HEADS * ATT_DH
OFF_MG = OFF_AG + BRANCH
IN_WIDTH = OFF_MG + MG_WIDTH
assert OFF_MG % GATE_BLOCK == 0 and D_MODEL % GATE_BLOCK == 0

VMEM_LIMIT = 56 * 1024 * 1024


def _sigmoid(x):
    return jax.nn.sigmoid(x)


def _silu(x):
    return x * jax.nn.sigmoid(x)


def _rms_norm_rows(x, gain):
    ms = jnp.mean(x * x, axis=-1, keepdims=True)
    return x * lax.rsqrt(ms + RMS_EPS) * gain


def _norm_kernel(x_ref, g_ref, h_ref):
    h_ref[...] = _rms_norm_rows(x_ref[...], g_ref[...]).astype(h_ref.dtype)


def _norm(x2, g, tm=512):
    m, d = x2.shape
    return pl.pallas_call(
        _norm_kernel,
        grid=(m // tm,),
        in_specs=[pl.BlockSpec((tm, d), lambda i: (i, 0)),
                  pl.BlockSpec((1, d), lambda i: (0, 0))],
        out_specs=pl.BlockSpec((tm, d), lambda i: (i, 0)),
        out_shape=jax.ShapeDtypeStruct((m, d), BF16),
        compiler_params=pltpu.CompilerParams(
            dimension_semantics=("arbitrary",), vmem_limit_bytes=VMEM_LIMIT),
        name="norm",
    )(x2, g)


def _inproj_kernel(h_ref, w_ref, z_ref):
    z_ref[...] = jnp.dot(h_ref[...], w_ref[...],
                         preferred_element_type=F32).astype(z_ref.dtype)


def _inproj(h, w_all, layer, tm=2048, tn=512):
    m, d = h.shape
    n = w_all.shape[2]
    return pl.pallas_call(
        _inproj_kernel,
        grid=(m // tm, n // tn),
        in_specs=[
            pl.BlockSpec((tm, d), lambda i, j: (i, 0)),
            pl.BlockSpec((None, d, tn), lambda i, j: (layer, 0, j)),
        ],
        out_specs=pl.BlockSpec((tm, tn), lambda i, j: (i, j)),
        out_shape=jax.ShapeDtypeStruct((m, n), BF16),
        compiler_params=pltpu.CompilerParams(
            dimension_semantics=("arbitrary", "arbitrary"),
            vmem_limit_bytes=VMEM_LIMIT),
        name="inproj",
    )(h, w_all)


_QF, _QB, _WF, _WB = range(4)


def _ret_kernel(af_ref, ab_ref, q_ref, k_ref, v_ref, cos_ref, sin_ref,
                o_ref, qx_s, kx_s, acc_s, sf_s, sb_s, dm_s, tab_s, *, seq):
    C = RET_CHUNK
    n_chunks = seq // C
    half = RET_DH // 2
    h = pl.program_id(1)

    lg_f = -jnp.exp(jnp.full((C, half), af_ref[h], F32))
    lg_b = -jnp.exp(jnp.full((C, half), ab_ref[h], F32))
    pos = lax.broadcasted_iota(jnp.int32, (C, half), 0).astype(F32)
    tab_s[_QF] = jnp.exp(lg_f * (pos + 1.0))
    tab_s[_QB] = jnp.exp(lg_b * (C - pos))
    tab_s[_WF] = jnp.exp(lg_f * (C - 1.0 - pos))
    tab_s[_WB] = jnp.exp(lg_b * pos)
    dec_f = jnp.exp(lg_f[:1] * C)
    dec_b = jnp.exp(lg_b[:1] * C)
    dec_f = jnp.concatenate([dec_f, dec_f], axis=1)
    dec_b = jnp.concatenate([dec_b, dec_b], axis=1)
    for c in range(C // half):
        row = lax.broadcasted_iota(jnp.int32, (C, half), 0)
        col = lax.broadcasted_iota(jnp.int32, (C, half), 1) + c * half
        lag = (row - col).astype(F32)
        alag = jnp.abs(lag)
        dm_s[:, c * half:(c + 1) * half] = jnp.where(
            lag >= 0, jnp.exp(lg_f * alag), jnp.exp(lg_b * alag))
    sf_s[...] = jnp.zeros_like(sf_s)
    sb_s[...] = jnp.zeros_like(sb_s)

    def rot(x, cos, sin):
        x1, x2 = x[:, :half], x[:, half:]
        return x1 * cos - x2 * sin, x2 * cos + x1 * sin

    def scaled16(x12, t):
        tab = tab_s[t]
        return jnp.concatenate([x12[0] * tab, x12[1] * tab], axis=1).astype(BF16)

    def update_state(s_ref, dec, k16, v):
        kv = lax.dot_general(k16, v, (((0,), (0,)), ((), ())), preferred_element_type=F32)
        s_ref[...] = s_ref[...] * dec + kv

    def first_visit(n, s_ref, dec, t_q, t_k, t_q_later, t_k_later):
        r = pl.ds(pl.multiple_of(n * C, C), C)
        cos = cos_ref[r, :]
        sin = sin_ref[r, :]
        q12 = rot(q_ref[r, :].astype(F32), cos, sin)
        k12 = rot(k_ref[r, :].astype(F32), cos, sin)
        k12 = (k12[0] * (RET_DH ** -0.5), k12[1] * (RET_DH ** -0.5))
        v = v_ref[r, :]
        q16 = jnp.concatenate(q12, axis=1).astype(BF16)
        k16 = jnp.concatenate(k12, axis=1).astype(BF16)
        scores = lax.dot_general(q16, k16, (((1,), (1,)), ((), ())),
                                 preferred_element_type=F32) * dm_s[...]
        o = jnp.dot(scores.astype(BF16), v, preferred_element_type=F32)
        o = o + jnp.dot(scaled16(q12, t_q), s_ref[...].astype(BF16),
                        preferred_element_type=F32)
        acc_s[r, :] = o
        update_state(s_ref, dec, scaled16(k12, t_k), v)
        qx_s[r, :] = scaled16(q12, t_q_later)
        kx_s[r, :] = scaled16(k12, t_k_later)

    def second_visit(n, s_ref, dec):
        r = pl.ds(pl.multiple_of(n * C, C), C)
        o = acc_s[r, :] + jnp.dot(qx_s[r, :], s_ref[...].astype(BF16),
                                  preferred_element_type=F32)
        y = o * lax.rsqrt(jnp.mean(o * o, axis=-1, keepdims=True) + RMS_EPS)
        o_ref[r, :] = y.astype(o_ref.dtype)
        update_state(s_ref, dec, kx_s[r, :], v_ref[r, :])

    def first_half(i, carry):
        first_visit(i, sf_s, dec_f, _QF, _WF, _QB, _WB)
        first_visit(n_chunks - 1 - i, sb_s, dec_b, _QB, _WB, _QF, _WF)
        return carry

    def second_half(i, carry):
        second_visit(i, sf_s, dec_f)
        second_visit(n_chunks - 1 - i, sb_s, dec_b)
        return carry

    lax.fori_loop(0, n_chunks // 2, first_half, 0)
    lax.fori_loop(n_chunks // 2, n_chunks, second_half, 0)


def _retention(z, a_f, a_b, cos, sin, batch, seq):
    dh = RET_DH
    assert seq % (2 * RET_CHUNK) == 0
    blk = lambda off: pl.BlockSpec((seq, dh), lambda b, h, o=off // dh: (b, o + h))
    smem = pl.BlockSpec(memory_space=pltpu.SMEM)
    tab = pl.BlockSpec((seq, dh // 2), lambda b, h: (0, 0), pipeline_mode=pl.Buffered(1))
    return pl.pallas_call(
        functools.partial(_ret_kernel, seq=seq),
        grid=(batch, RET_HEADS),
        in_specs=[smem, smem, blk(OFF_RQ), blk(OFF_RK), blk(OFF_RV), tab, tab],
        out_specs=pl.BlockSpec((seq, dh), lambda b, h: (b, h)),
        out_shape=jax.ShapeDtypeStruct((batch * seq, BRANCH), BF16),
        scratch_shapes=[
            pltpu.VMEM((seq, dh), BF16),
            pltpu.VMEM((seq, dh), BF16),
            pltpu.VMEM((seq, dh), F32),
            pltpu.VMEM((dh, dh), F32),
            pltpu.VMEM((dh, dh), F32),
            pltpu.VMEM((RET_CHUNK, RET_CHUNK), F32),
            pltpu.VMEM((4, RET_CHUNK, dh // 2), F32),
        ],
        compiler_params=pltpu.CompilerParams(
            dimension_semantics=("arbitrary", "arbitrary"),
            vmem_limit_bytes=VMEM_LIMIT),
        name="retention",
    )(a_f, a_b, z, z, z, cos, sin)


def _pool_kernel(u_ref, w_ref, sc_ref, o_ref, pad_s, *, seq, tile):
    H = POOL_HALO
    grp = pl.program_id(1)
    zeros = jnp.zeros((H, POOL_DG), F32)
    pad_s[0:H, :] = zeros
    pad_s[H + seq:H + seq + H, :] = zeros

    def fill(t, carry):
        r = pl.multiple_of(t * tile, tile)
        pad_s[pl.ds(H + r, tile), :] = u_ref[pl.ds(r, tile), :].astype(F32)
        return carry

    lax.fori_loop(0, seq // tile, fill, 0)

    w = w_ref[0]
    scale = sc_ref[...]

    for gi, win in enumerate(POOL_WINDOWS):
        @pl.when(grp == gi)
        def _(win=win):
            hw = win // 2

            def body(t, carry):
                r = pl.multiple_of(t * tile, tile)
                n_ext = tile + 2 * H
                ext = pad_s[pl.ds(r, n_ext), :]
                s = ext
                step = 1
                while step < win:
                    s = s + pltpu.roll(s, n_ext - step, 0)
                    step *= 2
                acc = pltpu.roll(s, hw, 0)[H:H + tile]
                pos = r + lax.broadcasted_iota(jnp.int32, (tile, POOL_DG), 0)
                lo = jnp.maximum(pos - hw, 0)
                hi = jnp.minimum(pos + hw, seq)
                cnt = (hi - lo).astype(F32)
                p = acc / cnt - ext[H:H + tile]
                y = jnp.dot(p.astype(BF16), w, preferred_element_type=F32) * scale
                o_ref[pl.ds(r, tile), :] = y.astype(o_ref.dtype)
                return carry

            lax.fori_loop(0, seq // tile, body, 0)


def _pool(z, pool_w16, pool_scale, layer, batch, seq, tile=256):
    dg = POOL_DG
    n_groups = len(POOL_WINDOWS)
    return pl.pallas_call(
        functools.partial(_pool_kernel, seq=seq, tile=tile),
        grid=(batch, n_groups),
        in_specs=[
            pl.BlockSpec((seq, dg), lambda b, g: (b, OFF_PV // dg + g)),
            pl.BlockSpec((None, 1, dg, dg), lambda b, g: (layer, g, 0, 0)),
            pl.BlockSpec((1, dg), lambda b, g: (0, g)),
        ],
        out_specs=pl.BlockSpec((seq, dg), lambda b, g: (b, g)),
        out_shape=jax.ShapeDtypeStruct((batch * seq, BRANCH), BF16),
        scratch_shapes=[pltpu.VMEM((seq + 2 * POOL_HALO, dg), F32)],
        compiler_params=pltpu.CompilerParams(
            dimension_semantics=("arbitrary", "arbitrary"),
            vmem_limit_bytes=VMEM_LIMIT),
        name="pool",
    )(z, pool_w16, pool_scale)


def _rope_partial(x, cos, sin_signed):
    lane = lax.broadcasted_iota(jnp.int32, x.shape, 1)
    up = pltpu.roll(x, ATT_DH - ROPE_DIMS // 2, 1)
    down = pltpu.roll(x, ROPE_DIMS // 2, 1)
    partner = jnp.where(lane < ROPE_DIMS // 2, up, down)
    return x * cos + partner * sin_signed


def _rms_rows(x, gain):
    return x * lax.rsqrt(jnp.mean(x * x, axis=-1, keepdims=True) + RMS_EPS) * gain


def _att_kernel(sink_ref, q_ref, k_ref, v_ref, qg_ref, kg_ref, cos_ref, sin_ref,
                o_ref, kn_s, *, seq, nq):
    L = ATT_BLOCK
    G = ATT_GROUP
    kk = pl.program_id(1)
    n = pl.program_id(2)
    n_blocks = seq // L

    @pl.when(n == 0)
    def _():
        def norm_k(t, carry):
            r = pl.ds(pl.multiple_of(t * L, L), L)
            kn = _rms_rows(k_ref[r, :].astype(F32), kg_ref[...])
            kn_s[r, :] = _rope_partial(kn, cos_ref[r, :], sin_ref[r, :]).astype(BF16)
            return carry
        lax.fori_loop(0, n_blocks, norm_k, 0)

    scale = ATT_DH ** -0.5
    exp2_coef = scale * 1.4426950408889634
    row = lax.broadcasted_iota(jnp.int32, (L, 3 * L), 0)
    col = lax.broadcasted_iota(jnp.int32, (L, 3 * L), 1)
    for blk in range(nq):
        nb = n * nq + blk
        qrows = pl.ds(pl.multiple_of(nb * L, L), L)
        cos = cos_ref[qrows, :]
        sin = sin_ref[qrows, :]
        rows = slice(blk * L, (blk + 1) * L)
        qs = []
        for h in range(G):
            qh = _rms_rows(q_ref[rows, h * ATT_DH:(h + 1) * ATT_DH].astype(F32), qg_ref[...])
            qs.append(_rope_partial(qh, cos, sin).astype(BF16))
        q = jnp.concatenate(qs, axis=0)

        start = jnp.clip((nb - 1) * L, 0, seq - 3 * L)
        start = pl.multiple_of(start, L)
        kw = kn_s[pl.ds(start, 3 * L), :]
        vw = v_ref[pl.ds(start, 3 * L), :]
        s = lax.dot_general(q, kw, (((1,), (1,)), ((), ())),
                            preferred_element_type=F32)
        valid = jnp.abs(col - row + (start - nb * L)) <= ATT_WINDOW
        outs = []
        for h in range(G):
            sh = jnp.where(valid, s[h * L:(h + 1) * L, :], NEG_BIG)
            sk = jnp.full((L, 1), sink_ref[kk * G + h] * (1.0 / scale), F32)
            m = jnp.maximum(jnp.max(sh, axis=-1, keepdims=True), sk)
            p = jnp.exp2((sh - m) * exp2_coef)
            denom = jnp.sum(p, axis=-1, keepdims=True) + jnp.exp2((sk - m) * exp2_coef)
            oh = jnp.dot(p.astype(BF16), vw, preferred_element_type=F32) / denom
            outs.append(oh.astype(o_ref.dtype))
        o_ref[rows, :] = jnp.concatenate(outs, axis=1)


def _attention(z, sink, q_gain, k_gain, cos, sin, batch, seq, nq=2):
    L = ATT_BLOCK
    n_blocks = seq // L
    assert n_blocks % nq == 0
    n_steps = n_blocks // nq
    gw = ATT_GROUP * ATT_DH
    smem = pl.BlockSpec(memory_space=pltpu.SMEM)
    tab = pl.BlockSpec((seq, ATT_DH), lambda b, k, n: (0, 0))
    gain = pl.BlockSpec((1, ATT_DH), lambda b, k, n: (0, 0))
    return pl.pallas_call(
        functools.partial(_att_kernel, seq=seq, nq=nq),
        grid=(batch, ATT_KV_HEADS, n_steps),
        in_specs=[
            smem,
            pl.BlockSpec((nq * L, gw), lambda b, k, n: (b * n_steps + n, OFF_AQ // gw + k)),
            pl.BlockSpec((seq, ATT_DH), lambda b, k, n: (b, OFF_AK // ATT_DH + k)),
            pl.BlockSpec((seq, ATT_DH), lambda b, k, n: (b, OFF_AV // ATT_DH + k)),
            gain, gain, tab, tab,
        ],
        out_specs=pl.BlockSpec((nq * L, gw), lambda b, k, n: (b * n_steps + n, k)),
        out_shape=jax.ShapeDtypeStruct((batch * seq, BRANCH), BF16),
        scratch_shapes=[pltpu.VMEM((seq, ATT_DH), BF16)],
        compiler_params=pltpu.CompilerParams(
            dimension_semantics=("arbitrary", "arbitrary", "arbitrary"),
            vmem_limit_bytes=VMEM_LIMIT),
        name="attention",
    )(sink, z, z, z, q_gain, k_gain, cos, sin)


def _merge_kernel(x_ref, ya_ref, yb_ref, yc_ref, rg_ref, pg_ref, ag0_ref, ag1_ref,
                  *rest, n_col, emit_next):
    gate_refs = rest[:N_GATES * n_col]
    rest = rest[N_GATES * n_col:]
    if emit_next:
        wa_ref, wb_ref, wc_ref, wo_ref, gn_ref, o_ref, hn_ref, m_s = rest
    else:
        wa_ref, wb_ref, wc_ref, wo_ref, o_ref, m_s = rest
    tc = GATE_BLOCK

    def gated(y, g):
        return (y.astype(F32) * _silu(g.astype(F32))).astype(BF16)

    half = BRANCH // 2
    ya = gated(ya_ref[...], rg_ref[...])
    yb = gated(yb_ref[...], pg_ref[...])
    yc = jnp.concatenate([gated(yc_ref[:, :half], ag0_ref[...]),
                          gated(yc_ref[:, half:], ag1_ref[...])], axis=1)
    for c in range(n_col):
        cs = slice(c * tc, (c + 1) * tc)
        a = jnp.dot(ya, wa_ref[:, cs], preferred_element_type=F32)
        b = jnp.dot(yb, wb_ref[:, cs], preferred_element_type=F32)
        cc = jnp.dot(yc, wc_ref[:, cs], preferred_element_type=F32)
        g0, g1, g2 = (gate_refs[j * n_col + c][...].astype(F32) for j in range(N_GATES))
        merged = _sigmoid(g0) * a + _sigmoid(g1) * b + _sigmoid(g2) * cc
        m_s[:, cs] = merged.astype(BF16)
    x_new = x_ref[...] + jnp.dot(m_s[...], wo_ref[...], preferred_element_type=F32)
    o_ref[...] = x_new
    if emit_next:
        hn_ref[...] = _rms_norm_rows(x_new, gn_ref[...]).astype(hn_ref.dtype)


def _merge(x2, ya, yb, yc, z, wa, wb, wc, wo, layer, g_next, tm=256):
    m, d = x2.shape
    n_col = d // GATE_BLOCK
    emit_next = g_next is not None
    row = lambda w: pl.BlockSpec((tm, w), lambda i: (i, 0))
    gate = lambda j, c: pl.BlockSpec(
        (tm, GATE_BLOCK), lambda i, o=OFF_MG // GATE_BLOCK + j * n_col + c: (i, o))
    gates = [gate(j, c) for j in range(N_GATES) for c in range(n_col)]
    wspec = lambda k: pl.BlockSpec((None, k, d), lambda i: (layer, 0, 0),
                                   pipeline_mode=pl.Buffered(1))
    zcols = lambda off, w: pl.BlockSpec((tm, w), lambda i, o=off // w: (i, o))
    half = BRANCH // 2
    in_specs = ([row(d), row(BRANCH), row(BRANCH), row(BRANCH),
                 zcols(OFF_RG, BRANCH), zcols(OFF_PG, BRANCH),
                 zcols(OFF_AG, half), zcols(OFF_AG + half, half)] + gates
                + [wspec(BRANCH), wspec(BRANCH), wspec(BRANCH), wspec(d)])
    args = [x2, ya, yb, yc] + [z] * (4 + len(gates)) + [wa, wb, wc, wo]
    out_specs = row(d)
    out_shape = jax.ShapeDtypeStruct((m, d), F32)
    if emit_next:
        in_specs.append(pl.BlockSpec((1, d), lambda i: (0, 0)))
        args.append(g_next)
        out_specs = (out_specs, row(d))
        out_shape = (out_shape, jax.ShapeDtypeStruct((m, d), BF16))
    return pl.pallas_call(
        functools.partial(_merge_kernel, n_col=n_col, emit_next=emit_next),
        grid=(m // tm,),
        in_specs=in_specs,
        out_specs=out_specs,
        out_shape=out_shape,
        scratch_shapes=[pltpu.VMEM((tm, d), BF16)],
        compiler_params=pltpu.CompilerParams(
            dimension_semantics=("arbitrary",),
            vmem_limit_bytes=VMEM_LIMIT),
        name="merge",
    )(*args)


def _ret_tables(seq):
    half = RET_DH // 2
    inv = 1.0 / (RET_ROPE_BASE ** jnp.linspace(0.0, 1.0, half, dtype=F32))
    ang = jnp.arange(seq, dtype=F32)[:, None] * inv[None, :]
    return jnp.cos(ang), jnp.sin(ang)


def _att_tables(seq):
    half = ROPE_DIMS // 2
    inv = ROPE_THETA ** (-jnp.arange(half, dtype=F32) / half)
    ang = jnp.arange(seq, dtype=F32)[:, None] * inv[None, :]
    c, s = jnp.cos(ang), jnp.sin(ang)
    rest = ATT_DH - ROPE_DIMS
    cos = jnp.concatenate([c, c, jnp.ones((seq, rest), F32)], axis=1)
    sin = jnp.concatenate([-s, s, jnp.zeros((seq, rest), F32)], axis=1)
    return cos, sin


def kernel(x, norm_g, w_in, ret_decay_fwd, ret_decay_bwd, pool_w, pool_scale,
           attn_q_gain, attn_k_gain, attn_sink, w_ret, w_pool, w_att, w_out):
    batch, seq, d = x.shape
    depth = w_in.shape[0]
    assert d == D_MODEL and w_in.shape[2] == IN_WIDTH
    assert seq >= 3 * ATT_BLOCK
    rcos, rsin = _ret_tables(seq)
    acos, asin = _att_tables(seq)
    w_in16, w_ret16, w_pool16, w_att16, w_out16, pool_w16 = (
        w.astype(BF16) for w in (w_in, w_ret, w_pool, w_att, w_out, pool_w))
    x2 = x.reshape(batch * seq, d)
    h = _norm(x2, norm_g[0][None, :])
    for l in range(depth):
        z = _inproj(h, w_in16, l)
        ya = _retention(z, ret_decay_fwd[l], ret_decay_bwd[l], rcos, rsin, batch, seq)
        yb = _pool(z, pool_w16, pool_scale[l][None, :], l, batch, seq)
        yc = _attention(z, attn_sink[l], attn_q_gain[l][None, :], attn_k_gain[l][None, :],
                        acos, asin, batch, seq)
        g_next = norm_g[l + 1][None, :] if l + 1 < depth else None
        res = _merge(x2, ya, yb, yc, z, w_ret16, w_pool16, w_att16, w_out16, l, g_next)
        x2, h = res if g_next is not None else (res, None)
    return x2.reshape(batch, seq, d)
```

```python
import functools

import jax
import jax.numpy as jnp
from jax import lax
from jax.experimental import pallas as pl
from jax.experimental.pallas import tpu as pltpu

F32 = jnp.float32
BF16 = jnp.bfloat16

D_MODEL = 2048
BRANCH = D_MODEL // 2
RET_HEADS = 4
RET_DH = BRANCH // RET_HEADS
RET_CHUNK = 256
RET_ROPE_BASE = 10000.0
POOL_WINDOWS = (2, 4, 8, 16)
POOL_DG = BRANCH // len(POOL_WINDOWS)
POOL_HALO = 16
ATT_DH = 128
ATT_Q_HEADS = BRANCH // ATT_DH
ATT_KV_HEADS = ATT_Q_HEADS // 4
ATT_GROUP = ATT_Q_HEADS // ATT_KV_HEADS
ATT_KV_WIDTH = ATT_KV_HEADS * ATT_DH
ATT_BLOCK = 128
ATT_WINDOW = 128
ROPE_THETA = 500000.0
ROPE_DIMS = ATT_DH // 4
RMS_EPS = 1e-6
NEG_BIG = -1e30
N_GATES = 3
MG_WIDTH = N_GATES * D_MODEL

W_AQ = 6 * BRANCH
W_AG = W_AQ + BRANCH + 2 * ATT_KV_WIDTH
IN_WIDTH = W_AG + BRANCH + MG_WIDTH

ZM_RQ, ZM_RK, ZM_RV, ZM_RG, ZM_PV, ZM_PG, ZM_AG, ZM_MG = (k * BRANCH for k in range(8))
ZM_WIDTH = ZM_MG + MG_WIDTH
ZA_WIDTH = BRANCH + 2 * ATT_KV_WIDTH
ZA_K = BRANCH
ZA_V = BRANCH + ATT_KV_WIDTH
GATE_BLOCK = 512

VMEM_LIMIT = 56 * 1024 * 1024


def _silu(x):
    return x * jax.nn.sigmoid(x)


def _rms_norm_rows(x, gain):
    ms = jnp.mean(x * x, axis=-1, keepdims=True)
    return x * lax.rsqrt(ms + RMS_EPS) * gain


def _norm_kernel(x_ref, g_ref, h_ref):
    h_ref[...] = _rms_norm_rows(x_ref[...], g_ref[...]).astype(h_ref.dtype)


def _norm(x2, g, tm=512):
    m, d = x2.shape
    return pl.pallas_call(
        _norm_kernel,
        grid=(m // tm,),
        in_specs=[pl.BlockSpec((tm, d), lambda i: (i, 0)),
                  pl.BlockSpec((1, d), lambda i: (0, 0))],
        out_specs=pl.BlockSpec((tm, d), lambda i: (i, 0)),
        out_shape=jax.ShapeDtypeStruct((m, d), BF16),
        compiler_params=pltpu.CompilerParams(
            dimension_semantics=("arbitrary",), vmem_limit_bytes=VMEM_LIMIT),
        name="norm",
    )(x2, g)


_GRID2 = pltpu.CompilerParams(dimension_semantics=("arbitrary", "arbitrary"),
                              vmem_limit_bytes=VMEM_LIMIT)


def _proj_main_kernel(h_ref, w_ref, z_ref):
    z_ref[...] = jnp.dot(h_ref[...], w_ref[...],
                         preferred_element_type=F32).astype(z_ref.dtype)


def _proj_main(h, w_all, layer, tm=2048, tn=512):
    m, d = h.shape
    n_before = W_AQ // tn
    skip = (W_AG - W_AQ) // tn
    return pl.pallas_call(
        _proj_main_kernel,
        grid=(m // tm, ZM_WIDTH // tn),
        in_specs=[
            pl.BlockSpec((tm, d), lambda i, j: (i, 0)),
            pl.BlockSpec((None, d, tn),
                         lambda i, j: (layer, 0, jnp.where(j < n_before, j, j + skip))),
        ],
        out_specs=pl.BlockSpec((tm, tn), lambda i, j: (i, j)),
        out_shape=jax.ShapeDtypeStruct((m, ZM_WIDTH), BF16),
        compiler_params=_GRID2,
        name="proj_main",
    )(h, w_all)


def _rope_partial(x, cos, sin_signed):
    lane = lax.broadcasted_iota(jnp.int32, x.shape, 1)
    up = pltpu.roll(x, ATT_DH - ROPE_DIMS // 2, 1)
    down = pltpu.roll(x, ROPE_DIMS // 2, 1)
    partner = jnp.where(lane < ROPE_DIMS // 2, up, down)
    return x * cos + partner * sin_signed


def _proj_att_kernel(h_ref, w_ref, gain_ref, cos_ref, sin_ref, z_ref, acc_s, *,
                     n_col, heads_per_tile, n_qk_tiles):
    t = pl.program_id(0)

    @pl.when(t == 0)
    def _():
        acc_s[...] = jnp.zeros_like(acc_s)

    is_qk = jnp.maximum(t - 1, 0) % n_col < n_qk_tiles
    prev = acc_s[...]
    acc_s[...] = jnp.dot(h_ref[...], w_ref[...], preferred_element_type=F32)
    cos = cos_ref[...]
    sin = sin_ref[...]
    outs = []
    for hh in range(heads_per_tile):
        cols = slice(hh * ATT_DH, (hh + 1) * ATT_DH)
        xh = prev[:, cols]
        qk = _rope_partial(_rms_norm_rows(xh, gain_ref[:, cols]), cos, sin)
        outs.append(jnp.where(is_qk, qk, xh))
    z_ref[...] = jnp.concatenate(outs, axis=1).astype(z_ref.dtype)


def _proj_att(h, w_all, layer, gains, cos, sin, seq, tm=2048, tn=256):
    m, d = h.shape
    n_col = ZA_WIDTH // tn
    n_tiles = (m // tm) * n_col
    pos_tiles = seq // tm

    def cur(t):
        tt = jnp.minimum(t, n_tiles - 1)
        return tt // n_col, tt % n_col

    def prev(t):
        tt = jnp.maximum(t - 1, 0)
        return tt // n_col, tt % n_col

    tab = pl.BlockSpec((tm, ATT_DH), lambda t: (prev(t)[0] % pos_tiles, 0))
    return pl.pallas_call(
        functools.partial(_proj_att_kernel, n_col=n_col, heads_per_tile=tn // ATT_DH,
                          n_qk_tiles=ZA_V // tn),
        grid=(n_tiles + 1,),
        in_specs=[
            pl.BlockSpec((tm, d), lambda t: (cur(t)[0], 0)),
            pl.BlockSpec((None, d, tn), lambda t: (layer, 0, W_AQ // tn + cur(t)[1])),
            pl.BlockSpec((1, tn), lambda t: (0, prev(t)[1])),
            tab, tab,
        ],
        out_specs=pl.BlockSpec((tm, tn), lambda t: prev(t)),
        out_shape=jax.ShapeDtypeStruct((m, ZA_WIDTH), BF16),
        scratch_shapes=[pltpu.VMEM((tm, tn), F32)],
        compiler_params=pltpu.CompilerParams(dimension_semantics=("arbitrary",),
                                             vmem_limit_bytes=VMEM_LIMIT),
        name="proj_att",
    )(h, w_all, gains, cos, sin)


_QF, _QB, _WF, _WB = range(4)


def _ret_kernel(af_ref, ab_ref, q_ref, k_ref, v_ref, g_ref, cos_ref, sin_ref,
                o_ref, qx_s, kx_s, acc_s, sf_s, sb_s, dm_s, tab_s, *, seq):
    C = RET_CHUNK
    n_chunks = seq // C
    half = RET_DH // 2
    h = pl.program_id(1)

    lg_f = -jnp.exp(jnp.full((C, half), af_ref[h], F32))
    lg_b = -jnp.exp(jnp.full((C, half), ab_ref[h], F32))
    pos = lax.broadcasted_iota(jnp.int32, (C, half), 0).astype(F32)
    tab_s[_QF] = jnp.exp(lg_f * (pos + 1.0))
    tab_s[_QB] = jnp.exp(lg_b * (C - pos))
    tab_s[_WF] = jnp.exp(lg_f * (C - 1.0 - pos))
    tab_s[_WB] = jnp.exp(lg_b * pos)
    dec_f = jnp.exp(lg_f[:1] * C)
    dec_b = jnp.exp(lg_b[:1] * C)
    dec_f = jnp.concatenate([dec_f, dec_f], axis=1)
    dec_b = jnp.concatenate([dec_b, dec_b], axis=1)
    for c in range(C // half):
        row = lax.broadcasted_iota(jnp.int32, (C, half), 0)
        col = lax.broadcasted_iota(jnp.int32, (C, half), 1) + c * half
        lag = (row - col).astype(F32)
        alag = jnp.abs(lag)
        dm_s[:, c * half:(c + 1) * half] = jnp.where(
            lag >= 0, jnp.exp(lg_f * alag), jnp.exp(lg_b * alag))
    sf_s[...] = jnp.zeros_like(sf_s)
    sb_s[...] = jnp.zeros_like(sb_s)

    def rot(x, cos, sin):
        x1, x2 = x[:, :half], x[:, half:]
        return x1 * cos - x2 * sin, x2 * cos + x1 * sin

    def scaled16(x12, t):
        tab = tab_s[t]
        return jnp.concatenate([x12[0] * tab, x12[1] * tab], axis=1).astype(BF16)

    def update_state(s_ref, dec, k16, v):
        kv = lax.dot_general(k16, v, (((0,), (0,)), ((), ())), preferred_element_type=F32)
        s_ref[...] = s_ref[...] * dec + kv

    def first_visit(n, s_ref, dec, t_q, t_k, t_q_later, t_k_later):
        r = pl.ds(pl.multiple_of(n * C, C), C)
        cos = cos_ref[r, :]
        sin = sin_ref[r, :]
        q12 = rot(q_ref[r, :].astype(F32), cos, sin)
        k12 = rot(k_ref[r, :].astype(F32), cos, sin)
        k12 = (k12[0] * (RET_DH ** -0.5), k12[1] * (RET_DH ** -0.5))
        v = v_ref[r, :]
        q16 = jnp.concatenate(q12, axis=1).astype(BF16)
        k16 = jnp.concatenate(k12, axis=1).astype(BF16)
        scores = lax.dot_general(q16, k16, (((1,), (1,)), ((), ())),
                                 preferred_element_type=F32) * dm_s[...]
        o = jnp.dot(scores.astype(BF16), v, preferred_element_type=F32)
        o = o + jnp.dot(scaled16(q12, t_q), s_ref[...].astype(BF16),
                        preferred_element_type=F32)
        acc_s[r, :] = o
        update_state(s_ref, dec, scaled16(k12, t_k), v)
        qx_s[r, :] = scaled16(q12, t_q_later)
        kx_s[r, :] = scaled16(k12, t_k_later)

    def second_visit(n, s_ref, dec):
        r = pl.ds(pl.multiple_of(n * C, C), C)
        o = acc_s[r, :] + jnp.dot(qx_s[r, :], s_ref[...].astype(BF16),
                                  preferred_element_type=F32)
        y = o * lax.rsqrt(jnp.mean(o * o, axis=-1, keepdims=True) + RMS_EPS)
        o_ref[r, :] = (y * _silu(g_ref[r, :].astype(F32))).astype(o_ref.dtype)
        update_state(s_ref, dec, kx_s[r, :], v_ref[r, :])

    def first_half(i, carry):
        first_visit(i, sf_s, dec_f, _QF, _WF, _QB, _WB)
        first_visit(n_chunks - 1 - i, sb_s, dec_b, _QB, _WB, _QF, _WF)
        return carry

    def second_half(i, carry):
        second_visit(i, sf_s, dec_f)
        second_visit(n_chunks - 1 - i, sb_s, dec_b)
        return carry

    lax.fori_loop(0, n_chunks // 2, first_half, 0)
    lax.fori_loop(n_chunks // 2, n_chunks, second_half, 0)


def _retention(zm, a_f, a_b, cos, sin, batch, seq):
    dh = RET_DH
    assert seq % (2 * RET_CHUNK) == 0
    blk = lambda off: pl.BlockSpec((seq, dh), lambda b, h, o=off // dh: (b, o + h))
    smem = pl.BlockSpec(memory_space=pltpu.SMEM)
    tab = pl.BlockSpec((seq, dh // 2), lambda b, h: (0, 0), pipeline_mode=pl.Buffered(1))
    return pl.pallas_call(
        functools.partial(_ret_kernel, seq=seq),
        grid=(batch, RET_HEADS),
        in_specs=[smem, smem, blk(ZM_RQ), blk(ZM_RK), blk(ZM_RV), blk(ZM_RG), tab, tab],
        out_specs=pl.BlockSpec((seq, dh), lambda b, h: (b, h)),
        out_shape=jax.ShapeDtypeStruct((batch * seq, BRANCH), BF16),
        scratch_shapes=[
            pltpu.VMEM((seq, dh), BF16),
            pltpu.VMEM((seq, dh), BF16),
            pltpu.VMEM((seq, dh), F32),
            pltpu.VMEM((dh, dh), F32),
            pltpu.VMEM((dh, dh), F32),
            pltpu.VMEM((RET_CHUNK, RET_CHUNK), F32),
            pltpu.VMEM((4, RET_CHUNK, dh // 2), F32),
        ],
        compiler_params=_GRID2,
        name="retention",
    )(a_f, a_b, zm, zm, zm, zm, cos, sin)


def _pool_kernel(u_ref, g_ref, w_ref, sc_ref, o_ref, pad_s, *, seq, tile):
    H = POOL_HALO
    grp = pl.program_id(1)
    zeros = jnp.zeros((H, POOL_DG), F32)
    pad_s[0:H, :] = zeros
    pad_s[H + seq:H + seq + H, :] = zeros

    def fill(t, carry):
        r = pl.multiple_of(t * tile, tile)
        pad_s[pl.ds(H + r, tile), :] = u_ref[pl.ds(r, tile), :].astype(F32)
        return carry

    lax.fori_loop(0, seq // tile, fill, 0)

    w = w_ref[0]
    scale = sc_ref[...]

    for gi, win in enumerate(POOL_WINDOWS):
        @pl.when(grp == gi)
        def _(win=win):
            hw = win // 2

            def body(t, carry):
                r = pl.multiple_of(t * tile, tile)
                n_ext = tile + 2 * H
                ext = pad_s[pl.ds(r, n_ext), :]
                s = ext
                step = 1
                while step < win:
                    s = s + pltpu.roll(s, n_ext - step, 0)
                    step *= 2
                acc = pltpu.roll(s, hw, 0)[H:H + tile]
                pos = r + lax.broadcasted_iota(jnp.int32, (tile, POOL_DG), 0)
                lo = jnp.maximum(pos - hw, 0)
                hi = jnp.minimum(pos + hw, seq)
                cnt = (hi - lo).astype(F32)
                p = acc / cnt - ext[H:H + tile]
                y = jnp.dot(p.astype(BF16), w, preferred_element_type=F32) * scale
                gate = _silu(g_ref[pl.ds(r, tile), :].astype(F32))
                o_ref[pl.ds(r, tile), :] = (y * gate).astype(o_ref.dtype)
                return carry

            lax.fori_loop(0, seq // tile, body, 0)


def _pool(zm, pool_w16, pool_scale, layer, batch, seq, tile=256):
    dg = POOL_DG
    n_groups = len(POOL_WINDOWS)
    return pl.pallas_call(
        functools.partial(_pool_kernel, seq=seq, tile=tile),
        grid=(batch, n_groups),
        in_specs=[
            pl.BlockSpec((seq, dg), lambda b, g: (b, ZM_PV // dg + g)),
            pl.BlockSpec((seq, dg), lambda b, g: (b, ZM_PG // dg + g)),
            pl.BlockSpec((None, 1, dg, dg), lambda b, g: (layer, g, 0, 0)),
            pl.BlockSpec((1, dg), lambda b, g: (0, g)),
        ],
        out_specs=pl.BlockSpec((seq, dg), lambda b, g: (b, g)),
        out_shape=jax.ShapeDtypeStruct((batch * seq, BRANCH), BF16),
        scratch_shapes=[pltpu.VMEM((seq + 2 * POOL_HALO, dg), F32)],
        compiler_params=_GRID2,
        name="pool",
    )(zm, zm, pool_w16, pool_scale)


def _att_kernel(sink_ref, q_ref, k_ref, v_ref, g_ref, o_ref, *, seq, nq):
    L = ATT_BLOCK
    G = ATT_GROUP
    kk = pl.program_id(1)
    n = pl.program_id(2)

    scale = ATT_DH ** -0.5
    exp2_coef = scale * 1.4426950408889634
    row = lax.broadcasted_iota(jnp.int32, (L, 3 * L), 0)
    col = lax.broadcasted_iota(jnp.int32, (L, 3 * L), 1)
    for blk in range(nq):
        nb = n * nq + blk
        rows = slice(blk * L, (blk + 1) * L)
        q = jnp.concatenate([q_ref[rows, h * ATT_DH:(h + 1) * ATT_DH] for h in range(G)],
                            axis=0)
        start = jnp.clip((nb - 1) * L, 0, seq - 3 * L)
        start = pl.multiple_of(start, L)
        kw = k_ref[pl.ds(start, 3 * L), :]
        vw = v_ref[pl.ds(start, 3 * L), :]
        s = lax.dot_general(q, kw, (((1,), (1,)), ((), ())),
                            preferred_element_type=F32)
        valid = jnp.abs(col - row + (start - nb * L)) <= ATT_WINDOW
        outs = []
        for h in range(G):
            sh = jnp.where(valid, s[h * L:(h + 1) * L, :], NEG_BIG)
            sk = jnp.full((L, 1), sink_ref[kk * G + h] * (1.0 / scale), F32)
            m = jnp.maximum(jnp.max(sh, axis=-1, keepdims=True), sk)
            p = jnp.exp2((sh - m) * exp2_coef)
            denom = jnp.sum(p, axis=-1, keepdims=True) + jnp.exp2((sk - m) * exp2_coef)
            oh = jnp.dot(p.astype(BF16), vw, preferred_element_type=F32) / denom
            gate = _silu(g_ref[rows, h * ATT_DH:(h + 1) * ATT_DH].astype(F32))
            outs.append((oh * gate).astype(o_ref.dtype))
        o_ref[rows, :] = jnp.concatenate(outs, axis=1)


def _attention(za, zm, sink, batch, seq, nq=2):
    L = ATT_BLOCK
    n_blocks = seq // L
    assert n_blocks % nq == 0
    n_steps = n_blocks // nq
    gw = ATT_GROUP * ATT_DH
    smem = pl.BlockSpec(memory_space=pltpu.SMEM)
    return pl.pallas_call(
        functools.partial(_att_kernel, seq=seq, nq=nq),
        grid=(batch, ATT_KV_HEADS, n_steps),
        in_specs=[
            smem,
            pl.BlockSpec((nq * L, gw), lambda b, k, n: (b * n_steps + n, k)),
            pl.BlockSpec((seq, ATT_DH), lambda b, k, n: (b, ZA_K // ATT_DH + k)),
            pl.BlockSpec((seq, ATT_DH), lambda b, k, n: (b, ZA_V // ATT_DH + k)),
            pl.BlockSpec((nq * L, gw), lambda b, k, n: (b * n_steps + n, ZM_AG // gw + k)),
        ],
        out_specs=pl.BlockSpec((nq * L, gw), lambda b, k, n: (b * n_steps + n, k)),
        out_shape=jax.ShapeDtypeStruct((batch * seq, BRANCH), BF16),
        compiler_params=pltpu.CompilerParams(
            dimension_semantics=("arbitrary", "arbitrary", "arbitrary"),
            vmem_limit_bytes=VMEM_LIMIT),
        name="attention",
    )(sink, za, za, za, zm)


def _merge_kernel(x_ref, ya_ref, yb_ref, yc_ref, *rest, n_col, emit_next):
    gate_refs = rest[:N_GATES * n_col]
    rest = rest[N_GATES * n_col:]
    if emit_next:
        wa_ref, wb_ref, wc_ref, wo_ref, gn_ref, o_ref, hn_ref, m_s = rest
    else:
        wa_ref, wb_ref, wc_ref, wo_ref, o_ref, m_s = rest
    tc = GATE_BLOCK
    ya = ya_ref[...]
    yb = yb_ref[...]
    yc = yc_ref[...]
    for c in range(n_col):
        cs = slice(c * tc, (c + 1) * tc)
        a = jnp.dot(ya, wa_ref[:, cs], preferred_element_type=F32)
        b = jnp.dot(yb, wb_ref[:, cs], preferred_element_type=F32)
        cc = jnp.dot(yc, wc_ref[:, cs], preferred_element_type=F32)
        g0, g1, g2 = (jax.nn.sigmoid(gate_refs[j * n_col + c][...].astype(F32))
                      for j in range(N_GATES))
        m_s[:, cs] = (g0 * a + g1 * b + g2 * cc).astype(BF16)
    x_new = x_ref[...] + jnp.dot(m_s[...], wo_ref[...], preferred_element_type=F32)
    o_ref[...] = x_new
    if emit_next:
        hn_ref[...] = _rms_norm_rows(x_new, gn_ref[...]).astype(hn_ref.dtype)


def _merge(x2, ya, yb, yc, zm, wa, wb, wc, wo, layer, g_next, tm=256):
    m, d = x2.shape
    n_col = d // GATE_BLOCK
    emit_next = g_next is not None
    row = lambda w: pl.BlockSpec((tm, w), lambda i: (i, 0))
    gate = lambda j, c: pl.BlockSpec(
        (tm, GATE_BLOCK), lambda i, o=ZM_MG // GATE_BLOCK + j * n_col + c: (i, o))
    gates = [gate(j, c) for j in range(N_GATES) for c in range(n_col)]
    wspec = lambda k: pl.BlockSpec((None, k, d), lambda i: (layer, 0, 0),
                                   pipeline_mode=pl.Buffered(1))
    in_specs = ([row(d), row(BRANCH), row(BRANCH), row(BRANCH)] + gates
                + [wspec(BRANCH), wspec(BRANCH), wspec(BRANCH), wspec(d)])
    args = [x2, ya, yb, yc] + [zm] * len(gates) + [wa, wb, wc, wo]
    out_specs = row(d)
    out_shape = jax.ShapeDtypeStruct((m, d), F32)
    if emit_next:
        in_specs.append(pl.BlockSpec((1, d), lambda i: (0, 0)))
        args.append(g_next)
        out_specs = (out_specs, row(d))
        out_shape = (out_shape, jax.ShapeDtypeStruct((m, d), BF16))
    return pl.pallas_call(
        functools.partial(_merge_kernel, n_col=n_col, emit_next=emit_next),
        grid=(m // tm,),
        in_specs=in_specs,
        out_specs=out_specs,
        out_shape=out_shape,
        scratch_shapes=[pltpu.VMEM((tm, d), BF16)],
        compiler_params=pltpu.CompilerParams(
            dimension_semantics=("arbitrary",),
            vmem_limit_bytes=VMEM_LIMIT),
        name="merge",
    )(*args)


def _ret_tables(seq):
    half = RET_DH // 2
    inv = 1.0 / (RET_ROPE_BASE ** jnp.linspace(0.0, 1.0, half, dtype=F32))
    ang = jnp.arange(seq, dtype=F32)[:, None] * inv[None, :]
    return jnp.cos(ang), jnp.sin(ang)


def _att_tables(seq):
    half = ROPE_DIMS // 2
    inv = ROPE_THETA ** (-jnp.arange(half, dtype=F32) / half)
    ang = jnp.arange(seq, dtype=F32)[:, None] * inv[None, :]
    c, s = jnp.cos(ang), jnp.sin(ang)
    rest = ATT_DH - ROPE_DIMS
    cos = jnp.concatenate([c, c, jnp.ones((seq, rest), F32)], axis=1)
    sin = jnp.concatenate([-s, s, jnp.zeros((seq, rest), F32)], axis=1)
    return cos, sin


def kernel(x, norm_g, w_in, ret_decay_fwd, ret_decay_bwd, pool_w, pool_scale,
           attn_q_gain, attn_k_gain, attn_sink, w_ret, w_pool, w_att, w_out):
    batch, seq, d = x.shape
    depth = w_in.shape[0]
    assert d == D_MODEL and w_in.shape[2] == IN_WIDTH
    assert seq >= 3 * ATT_BLOCK
    rcos, rsin = _ret_tables(seq)
    acos, asin = _att_tables(seq)
    w_in16, w_ret16, w_pool16, w_att16, w_out16, pool_w16 = (
        w.astype(BF16) for w in (w_in, w_ret, w_pool, w_att, w_out, pool_w))
    x2 = x.reshape(batch * seq, d)
    h = _norm(x2, norm_g[0][None, :])
    for l in range(depth):
        qk_gains = jnp.concatenate(
            [jnp.tile(attn_q_gain[l], ATT_Q_HEADS), jnp.tile(attn_k_gain[l], ATT_KV_HEADS),
             jnp.ones((ATT_KV_WIDTH,), F32)])[None, :]
        za = _proj_att(h, w_in16, l, qk_gains, acos, asin, seq)
        zm = _proj_main(h, w_in16, l)
        ya = _retention(zm, ret_decay_fwd[l], ret_decay_bwd[l], rcos, rsin, batch, seq)
        yb = _pool(zm, pool_w16, pool_scale[l][None, :], l, batch, seq)
        yc = _attention(za, zm, attn_sink[l], batch, seq)
        g_next = norm_g[l + 1][None, :] if l + 1 < depth else None
        res = _merge(x2, ya, yb, yc, zm, w_ret16, w_pool16, w_att16, w_out16, l, g_next)
        x2, h = res if g_next is not None else (res, None)
    return x2.reshape(batch, seq, d)
```

```python
import functools

import jax
import jax.numpy as jnp
from jax import lax
from jax.experimental import pallas as pl
from jax.experimental.pallas import tpu as pltpu

F32 = jnp.float32
BF16 = jnp.bfloat16

D_MODEL = 2048
BRANCH = D_MODEL // 2
RET_HEADS = 4
RET_DH = BRANCH // RET_HEADS
RET_CHUNK = 256
RET_ROPE_BASE = 10000.0
POOL_WINDOWS = (2, 4, 8, 16)
POOL_DG = BRANCH // len(POOL_WINDOWS)
POOL_HALO = 16
ATT_DH = 128
ATT_Q_HEADS = BRANCH // ATT_DH
ATT_KV_HEADS = ATT_Q_HEADS // 4
ATT_GROUP = ATT_Q_HEADS // ATT_KV_HEADS
ATT_KV_WIDTH = ATT_KV_HEADS * ATT_DH
ATT_BLOCK = 128
ATT_WINDOW = 128
ROPE_THETA = 500000.0
ROPE_DIMS = ATT_DH // 4
RMS_EPS = 1e-6
NEG_BIG = -1e30
N_GATES = 3
MG_WIDTH = N_GATES * D_MODEL

W_AQ = 6 * BRANCH
W_AG = W_AQ + BRANCH + 2 * ATT_KV_WIDTH
IN_WIDTH = W_AG + BRANCH + MG_WIDTH

ZM_RQ, ZM_RK, ZM_RV, ZM_RG, ZM_PV, ZM_PG, ZM_AG, ZM_MG = (k * BRANCH for k in range(8))
ZM_WIDTH = ZM_MG + MG_WIDTH
ZA_WIDTH = BRANCH + 2 * ATT_KV_WIDTH
ZA_K = BRANCH
ZA_V = BRANCH + ATT_KV_WIDTH
GATE_BLOCK = 512

VMEM_LIMIT = 56 * 1024 * 1024


def _silu(x):
    return x * jax.nn.sigmoid(x)


def _rms_norm_rows(x, gain):
    ms = jnp.mean(x * x, axis=-1, keepdims=True)
    return x * lax.rsqrt(ms + RMS_EPS) * gain


def _norm_kernel(x_ref, g_ref, h_ref):
    h_ref[...] = _rms_norm_rows(x_ref[...], g_ref[...]).astype(h_ref.dtype)


def _norm(x2, g, tm=512):
    m, d = x2.shape
    return pl.pallas_call(
        _norm_kernel,
        grid=(m // tm,),
        in_specs=[pl.BlockSpec((tm, d), lambda i: (i, 0)),
                  pl.BlockSpec((1, d), lambda i: (0, 0))],
        out_specs=pl.BlockSpec((tm, d), lambda i: (i, 0)),
        out_shape=jax.ShapeDtypeStruct((m, d), BF16),
        compiler_params=pltpu.CompilerParams(
            dimension_semantics=("arbitrary",), vmem_limit_bytes=VMEM_LIMIT),
        name="norm",
    )(x2, g)


_GRID2 = pltpu.CompilerParams(dimension_semantics=("arbitrary", "arbitrary"),
                              vmem_limit_bytes=VMEM_LIMIT)


def _proj_main_kernel(h_ref, *refs):
    w_refs, z_ref = refs[:-1], refs[-1]
    tn = w_refs[0].shape[1]
    h = h_ref[...]
    for k, w_ref in enumerate(w_refs):
        z_ref[:, k * tn:(k + 1) * tn] = jnp.dot(
            h, w_ref[...], preferred_element_type=F32).astype(z_ref.dtype)


def _proj_main(h, w_all, layer, tm=2048, tn=512, tiles_per_step=2):
    m, d = h.shape
    n_before = W_AQ // tn
    skip = (W_AG - W_AQ) // tn
    n_tiles = ZM_WIDTH // tn
    assert n_tiles % tiles_per_step == 0

    def w_spec(k):
        def index(i, j):
            tile = j * tiles_per_step + k
            return layer, 0, jnp.where(tile < n_before, tile, tile + skip)
        return pl.BlockSpec((None, d, tn), index)

    return pl.pallas_call(
        _proj_main_kernel,
        grid=(m // tm, n_tiles // tiles_per_step),
        in_specs=[pl.BlockSpec((tm, d), lambda i, j: (i, 0))]
                 + [w_spec(k) for k in range(tiles_per_step)],
        out_specs=pl.BlockSpec((tm, tiles_per_step * tn), lambda i, j: (i, j)),
        out_shape=jax.ShapeDtypeStruct((m, ZM_WIDTH), BF16),
        compiler_params=_GRID2,
        name="proj_main",
    )(h, *([w_all] * tiles_per_step))


def _rope_partial(x, cos, sin_signed):
    lane = lax.broadcasted_iota(jnp.int32, x.shape, 1)
    up = pltpu.roll(x, ATT_DH - ROPE_DIMS // 2, 1)
    down = pltpu.roll(x, ROPE_DIMS // 2, 1)
    partner = jnp.where(lane < ROPE_DIMS // 2, up, down)
    return x * cos + partner * sin_signed


def _proj_att_kernel(h_ref, w_ref, gain_ref, cos_ref, sin_ref, z_ref, acc_s, *,
                     n_col, heads_per_tile, n_qk_tiles):
    t = pl.program_id(0)

    @pl.when(t == 0)
    def _():
        acc_s[...] = jnp.zeros_like(acc_s)

    is_qk = jnp.maximum(t - 1, 0) % n_col < n_qk_tiles
    prev = acc_s[...]
    acc_s[...] = jnp.dot(h_ref[...], w_ref[...], preferred_element_type=F32)
    cos = cos_ref[...]
    sin = sin_ref[...]
    outs = []
    for hh in range(heads_per_tile):
        cols = slice(hh * ATT_DH, (hh + 1) * ATT_DH)
        xh = prev[:, cols]
        qk = _rope_partial(_rms_norm_rows(xh, gain_ref[:, cols]), cos, sin)
        outs.append(jnp.where(is_qk, qk, xh))
    z_ref[...] = jnp.concatenate(outs, axis=1).astype(z_ref.dtype)


def _proj_att(h, w_all, layer, gains, cos, sin, seq, tm=2048, tn=256):
    m, d = h.shape
    n_col = ZA_WIDTH // tn
    n_tiles = (m // tm) * n_col
    pos_tiles = seq // tm

    def cur(t):
        tt = jnp.minimum(t, n_tiles - 1)
        return tt // n_col, tt % n_col

    def prev(t):
        tt = jnp.maximum(t - 1, 0)
        return tt // n_col, tt % n_col

    tab = pl.BlockSpec((tm, ATT_DH), lambda t: (prev(t)[0] % pos_tiles, 0))
    return pl.pallas_call(
        functools.partial(_proj_att_kernel, n_col=n_col, heads_per_tile=tn // ATT_DH,
                          n_qk_tiles=ZA_V // tn),
        grid=(n_tiles + 1,),
        in_specs=[
            pl.BlockSpec((tm, d), lambda t: (cur(t)[0], 0)),
            pl.BlockSpec((None, d, tn), lambda t: (layer, 0, W_AQ // tn + cur(t)[1])),
            pl.BlockSpec((1, tn), lambda t: (0, prev(t)[1])),
            tab, tab,
        ],
        out_specs=pl.BlockSpec((tm, tn), lambda t: prev(t)),
        out_shape=jax.ShapeDtypeStruct((m, ZA_WIDTH), BF16),
        scratch_shapes=[pltpu.VMEM((tm, tn), F32)],
        compiler_params=pltpu.CompilerParams(dimension_semantics=("arbitrary",),
                                             vmem_limit_bytes=VMEM_LIMIT),
        name="proj_att",
    )(h, w_all, gains, cos, sin)


_QF, _QB, _WF, _WB = range(4)


def _ret_kernel(af_ref, ab_ref, q_ref, k_ref, v_ref, g_ref, cos_ref, sin_ref,
                o_ref, qx_s, kx_s, acc_s, sf_s, sb_s, dm_s, tab_s, *, seq):
    C = RET_CHUNK
    n_chunks = seq // C
    half = RET_DH // 2
    h = pl.program_id(1)

    lg_f = -jnp.exp(jnp.full((C, half), af_ref[h], F32))
    lg_b = -jnp.exp(jnp.full((C, half), ab_ref[h], F32))
    pos = lax.broadcasted_iota(jnp.int32, (C, half), 0).astype(F32)
    tab_s[_QF] = jnp.exp(lg_f * (pos + 1.0))
    tab_s[_QB] = jnp.exp(lg_b * (C - pos))
    tab_s[_WF] = jnp.exp(lg_f * (C - 1.0 - pos))
    tab_s[_WB] = jnp.exp(lg_b * pos)
    dec_f = jnp.exp(lg_f[:1] * C)
    dec_b = jnp.exp(lg_b[:1] * C)
    dec_f = jnp.concatenate([dec_f, dec_f], axis=1)
    dec_b = jnp.concatenate([dec_b, dec_b], axis=1)
    for c in range(C // half):
        row = lax.broadcasted_iota(jnp.int32, (C, half), 0)
        col = lax.broadcasted_iota(jnp.int32, (C, half), 1) + c * half
        lag = (row - col).astype(F32)
        alag = jnp.abs(lag)
        dm_s[:, c * half:(c + 1) * half] = jnp.where(
            lag >= 0, jnp.exp(lg_f * alag), jnp.exp(lg_b * alag))
    sf_s[...] = jnp.zeros_like(sf_s)
    sb_s[...] = jnp.zeros_like(sb_s)

    def rot(x, cos, sin):
        x1, x2 = x[:, :half], x[:, half:]
        return x1 * cos - x2 * sin, x2 * cos + x1 * sin

    def scaled16(x12, t):
        tab = tab_s[t]
        return jnp.concatenate([x12[0] * tab, x12[1] * tab], axis=1).astype(BF16)

    def update_state(s_ref, dec, k16, v):
        kv = lax.dot_general(k16, v, (((0,), (0,)), ((), ())), preferred_element_type=F32)
        s_ref[...] = s_ref[...] * dec + kv

    def first_visit(n, s_ref, dec, t_q, t_k, t_q_later, t_k_later):
        r = pl.ds(pl.multiple_of(n * C, C), C)
        cos = cos_ref[r, :]
        sin = sin_ref[r, :]
        q12 = rot(q_ref[r, :].astype(F32), cos, sin)
        k12 = rot(k_ref[r, :].astype(F32), cos, sin)
        k12 = (k12[0] * (RET_DH ** -0.5), k12[1] * (RET_DH ** -0.5))
        v = v_ref[r, :]
        q16 = jnp.concatenate(q12, axis=1).astype(BF16)
        k16 = jnp.concatenate(k12, axis=1).astype(BF16)
        scores = lax.dot_general(q16, k16, (((1,), (1,)), ((), ())),
                                 preferred_element_type=F32) * dm_s[...]
        o = jnp.dot(scores.astype(BF16), v, preferred_element_type=F32)
        o = o + jnp.dot(scaled16(q12, t_q), s_ref[...].astype(BF16),
                        preferred_element_type=F32)
        acc_s[r, :] = o
        update_state(s_ref, dec, scaled16(k12, t_k), v)
        qx_s[r, :] = scaled16(q12, t_q_later)
        kx_s[r, :] = scaled16(k12, t_k_later)

    def second_visit(n, s_ref, dec):
        r = pl.ds(pl.multiple_of(n * C, C), C)
        o = acc_s[r, :] + jnp.dot(qx_s[r, :], s_ref[...].astype(BF16),
                                  preferred_element_type=F32)
        y = o * lax.rsqrt(jnp.mean(o * o, axis=-1, keepdims=True) + RMS_EPS)
        o_ref[r, :] = (y * _silu(g_ref[r, :].astype(F32))).astype(o_ref.dtype)
        update_state(s_ref, dec, kx_s[r, :], v_ref[r, :])

    def first_half(i, carry):
        first_visit(i, sf_s, dec_f, _QF, _WF, _QB, _WB)
        first_visit(n_chunks - 1 - i, sb_s, dec_b, _QB, _WB, _QF, _WF)
        return carry

    def second_half(i, carry):
        second_visit(i, sf_s, dec_f)
        second_visit(n_chunks - 1 - i, sb_s, dec_b)
        return carry

    lax.fori_loop(0, n_chunks // 2, first_half, 0, unroll=4)
    lax.fori_loop(n_chunks // 2, n_chunks, second_half, 0, unroll=4)


def _retention(zm, a_f, a_b, cos, sin, batch, seq):
    dh = RET_DH
    assert seq % (2 * RET_CHUNK) == 0
    blk = lambda off: pl.BlockSpec((seq, dh), lambda b, h, o=off // dh: (b, o + h))
    smem = pl.BlockSpec(memory_space=pltpu.SMEM)
    tab = pl.BlockSpec((seq, dh // 2), lambda b, h: (0, 0), pipeline_mode=pl.Buffered(1))
    return pl.pallas_call(
        functools.partial(_ret_kernel, seq=seq),
        grid=(batch, RET_HEADS),
        in_specs=[smem, smem, blk(ZM_RQ), blk(ZM_RK), blk(ZM_RV), blk(ZM_RG), tab, tab],
        out_specs=pl.BlockSpec((seq, dh), lambda b, h: (b, h)),
        out_shape=jax.ShapeDtypeStruct((batch * seq, BRANCH), BF16),
        scratch_shapes=[
            pltpu.VMEM((seq, dh), BF16),
            pltpu.VMEM((seq, dh), BF16),
            pltpu.VMEM((seq, dh), F32),
            pltpu.VMEM((dh, dh), F32),
            pltpu.VMEM((dh, dh), F32),
            pltpu.VMEM((RET_CHUNK, RET_CHUNK), F32),
            pltpu.VMEM((4, RET_CHUNK, dh // 2), F32),
        ],
        compiler_params=_GRID2,
        name="retention",
    )(a_f, a_b, zm, zm, zm, zm, cos, sin)


def _pool_kernel(u_ref, g_ref, w_ref, sc_ref, o_ref, pad_s, *, seq, tile):
    H = POOL_HALO
    grp = pl.program_id(1)
    zeros = jnp.zeros((H, POOL_DG), F32)
    pad_s[0:H, :] = zeros
    pad_s[H + seq:H + seq + H, :] = zeros

    def fill(t, carry):
        r = pl.multiple_of(t * tile, tile)
        pad_s[pl.ds(H + r, tile), :] = u_ref[pl.ds(r, tile), :].astype(F32)
        return carry

    lax.fori_loop(0, seq // tile, fill, 0)

    w = w_ref[0]
    scale = sc_ref[...]

    for gi, win in enumerate(POOL_WINDOWS):
        @pl.when(grp == gi)
        def _(win=win):
            hw = win // 2

            def body(t, carry):
                r = pl.multiple_of(t * tile, tile)
                n_ext = tile + 2 * H
                ext = pad_s[pl.ds(r, n_ext), :]
                s = ext
                step = 1
                while step < win:
                    s = s + pltpu.roll(s, n_ext - step, 0)
                    step *= 2
                acc = pltpu.roll(s, hw, 0)[H:H + tile]
                pos = r + lax.broadcasted_iota(jnp.int32, (tile, POOL_DG), 0)
                lo = jnp.maximum(pos - hw, 0)
                hi = jnp.minimum(pos + hw, seq)
                cnt = (hi - lo).astype(F32)
                p = acc / cnt - ext[H:H + tile]
                y = jnp.dot(p.astype(BF16), w, preferred_element_type=F32) * scale
                gate = _silu(g_ref[pl.ds(r, tile), :].astype(F32))
                o_ref[pl.ds(r, tile), :] = (y * gate).astype(o_ref.dtype)
                return carry

            lax.fori_loop(0, seq // tile, body, 0)


def _pool(zm, pool_w16, pool_scale, layer, batch, seq, tile=256):
    dg = POOL_DG
    n_groups = len(POOL_WINDOWS)
    return pl.pallas_call(
        functools.partial(_pool_kernel, seq=seq, tile=tile),
        grid=(batch, n_groups),
        in_specs=[
            pl.BlockSpec((seq, dg), lambda b, g: (b, ZM_PV // dg + g)),
            pl.BlockSpec((seq, dg), lambda b, g: (b, ZM_PG // dg + g)),
            pl.BlockSpec((None, 1, dg, dg), lambda b, g: (layer, g, 0, 0)),
            pl.BlockSpec((1, dg), lambda b, g: (0, g)),
        ],
        out_specs=pl.BlockSpec((seq, dg), lambda b, g: (b, g)),
        out_shape=jax.ShapeDtypeStruct((batch * seq, BRANCH), BF16),
        scratch_shapes=[pltpu.VMEM((seq + 2 * POOL_HALO, dg), F32)],
        compiler_params=_GRID2,
        name="pool",
    )(zm, zm, pool_w16, pool_scale)


def _att_kernel(sink_ref, q_ref, k_ref, v_ref, g_ref, o_ref, *, seq, nq):
    L = ATT_BLOCK
    G = ATT_GROUP
    kk = pl.program_id(1)
    n = pl.program_id(2)

    scale = ATT_DH ** -0.5
    exp2_coef = scale * 1.4426950408889634
    row = lax.broadcasted_iota(jnp.int32, (L, 3 * L), 0)
    col = lax.broadcasted_iota(jnp.int32, (L, 3 * L), 1)
    for blk in range(nq):
        nb = n * nq + blk
        rows = slice(blk * L, (blk + 1) * L)
        q = jnp.concatenate([q_ref[rows, h * ATT_DH:(h + 1) * ATT_DH] for h in range(G)],
                            axis=0)
        start = jnp.clip((nb - 1) * L, 0, seq - 3 * L)
        start = pl.multiple_of(start, L)
        kw = k_ref[pl.ds(start, 3 * L), :]
        vw = v_ref[pl.ds(start, 3 * L), :]
        s = lax.dot_general(q, kw, (((1,), (1,)), ((), ())),
                            preferred_element_type=F32)
        valid = jnp.abs(col - row + (start - nb * L)) <= ATT_WINDOW
        outs = []
        for h in range(G):
            sh = jnp.where(valid, s[h * L:(h + 1) * L, :], NEG_BIG)
            sk = jnp.full((L, 1), sink_ref[kk * G + h] * (1.0 / scale), F32)
            m = jnp.maximum(jnp.max(sh, axis=-1, keepdims=True), sk)
            p = jnp.exp2((sh - m) * exp2_coef)
            denom = jnp.sum(p, axis=-1, keepdims=True) + jnp.exp2((sk - m) * exp2_coef)
            oh = jnp.dot(p.astype(BF16), vw, preferred_element_type=F32) / denom
            gate = _silu(g_ref[rows, h * ATT_DH:(h + 1) * ATT_DH].astype(F32))
            outs.append((oh * gate).astype(o_ref.dtype))
        o_ref[rows, :] = jnp.concatenate(outs, axis=1)


def _attention(za, zm, sink, batch, seq, nq=8):
    L = ATT_BLOCK
    n_blocks = seq // L
    assert n_blocks % nq == 0
    n_steps = n_blocks // nq
    gw = ATT_GROUP * ATT_DH
    smem = pl.BlockSpec(memory_space=pltpu.SMEM)
    return pl.pallas_call(
        functools.partial(_att_kernel, seq=seq, nq=nq),
        grid=(batch, ATT_KV_HEADS, n_steps),
        in_specs=[
            smem,
            pl.BlockSpec((nq * L, gw), lambda b, k, n: (b * n_steps + n, k)),
            pl.BlockSpec((seq, ATT_DH), lambda b, k, n: (b, ZA_K // ATT_DH + k)),
            pl.BlockSpec((seq, ATT_DH), lambda b, k, n: (b, ZA_V // ATT_DH + k)),
            pl.BlockSpec((nq * L, gw), lambda b, k, n: (b * n_steps + n, ZM_AG // gw + k)),
        ],
        out_specs=pl.BlockSpec((nq * L, gw), lambda b, k, n: (b * n_steps + n, k)),
        out_shape=jax.ShapeDtypeStruct((batch * seq, BRANCH), BF16),
        compiler_params=pltpu.CompilerParams(
            dimension_semantics=("arbitrary", "arbitrary", "arbitrary"),
            vmem_limit_bytes=VMEM_LIMIT),
        name="attention",
    )(sink, za, za, za, zm)


def _merge_kernel(x_ref, ya_ref, yb_ref, yc_ref, *rest, n_col, emit_next):
    gate_refs = rest[:N_GATES * n_col]
    rest = rest[N_GATES * n_col:]
    if emit_next:
        wa_ref, wb_ref, wc_ref, wo_ref, gn_ref, o_ref, hn_ref, m_s = rest
    else:
        wa_ref, wb_ref, wc_ref, wo_ref, o_ref, m_s = rest
    tc = GATE_BLOCK
    ya = ya_ref[...]
    yb = yb_ref[...]
    yc = yc_ref[...]
    for c in range(n_col):
        cs = slice(c * tc, (c + 1) * tc)
        a = jnp.dot(ya, wa_ref[:, cs], preferred_element_type=F32)
        b = jnp.dot(yb, wb_ref[:, cs], preferred_element_type=F32)
        cc = jnp.dot(yc, wc_ref[:, cs], preferred_element_type=F32)
        g0, g1, g2 = (jax.nn.sigmoid(gate_refs[j * n_col + c][...].astype(F32))
                      for j in range(N_GATES))
        m_s[:, cs] = (g0 * a + g1 * b + g2 * cc).astype(BF16)
    x_new = x_ref[...] + jnp.dot(m_s[...], wo_ref[...], preferred_element_type=F32)
    o_ref[...] = x_new
    if emit_next:
        hn_ref[...] = _rms_norm_rows(x_new, gn_ref[...]).astype(hn_ref.dtype)


def _merge(x2, ya, yb, yc, zm, wa, wb, wc, wo, layer, g_next, tm=256):
    m, d = x2.shape
    n_col = d // GATE_BLOCK
    emit_next = g_next is not None
    row = lambda w: pl.BlockSpec((tm, w), lambda i: (i, 0))
    gate = lambda j, c: pl.BlockSpec(
        (tm, GATE_BLOCK), lambda i, o=ZM_MG // GATE_BLOCK + j * n_col + c: (i, o))
    gates = [gate(j, c) for j in range(N_GATES) for c in range(n_col)]
    wspec = lambda k: pl.BlockSpec((None, k, d), lambda i: (layer, 0, 0),
                                   pipeline_mode=pl.Buffered(1))
    in_specs = ([row(d), row(BRANCH), row(BRANCH), row(BRANCH)] + gates
                + [wspec(BRANCH), wspec(BRANCH), wspec(BRANCH), wspec(d)])
    args = [x2, ya, yb, yc] + [zm] * len(gates) + [wa, wb, wc, wo]
    out_specs = row(d)
    out_shape = jax.ShapeDtypeStruct((m, d), F32)
    if emit_next:
        in_specs.append(pl.BlockSpec((1, d), lambda i: (0, 0)))
        args.append(g_next)
        out_specs = (out_specs, row(d))
        out_shape = (out_shape, jax.ShapeDtypeStruct((m, d), BF16))
    return pl.pallas_call(
        functools.partial(_merge_kernel, n_col=n_col, emit_next=emit_next),
        grid=(m // tm,),
        in_specs=in_specs,
        out_specs=out_specs,
        out_shape=out_shape,
        scratch_shapes=[pltpu.VMEM((tm, d), BF16)],
        compiler_params=pltpu.CompilerParams(
            dimension_semantics=("arbitrary",),
            vmem_limit_bytes=VMEM_LIMIT),
        name="merge",
    )(*args)


def _ret_tables(seq):
    half = RET_DH // 2
    inv = 1.0 / (RET_ROPE_BASE ** jnp.linspace(0.0, 1.0, half, dtype=F32))
    ang = jnp.arange(seq, dtype=F32)[:, None] * inv[None, :]
    return jnp.cos(ang), jnp.sin(ang)


def _att_tables(seq):
    half = ROPE_DIMS // 2
    inv = ROPE_THETA ** (-jnp.arange(half, dtype=F32) / half)
    ang = jnp.arange(seq, dtype=F32)[:, None] * inv[None, :]
    c, s = jnp.cos(ang), jnp.sin(ang)
    rest = ATT_DH - ROPE_DIMS
    cos = jnp.concatenate([c, c, jnp.ones((seq, rest), F32)], axis=1)
    sin = jnp.concatenate([-s, s, jnp.zeros((seq, rest), F32)], axis=1)
    return cos, sin


def kernel(x, norm_g, w_in, ret_decay_fwd, ret_decay_bwd, pool_w, pool_scale,
           attn_q_gain, attn_k_gain, attn_sink, w_ret, w_pool, w_att, w_out):
    batch, seq, d = x.shape
    depth = w_in.shape[0]
    assert d == D_MODEL and w_in.shape[2] == IN_WIDTH
    assert seq >= 3 * ATT_BLOCK
    rcos, rsin = _ret_tables(seq)
    acos, asin = _att_tables(seq)
    w_in16, w_ret16, w_pool16, w_att16, w_out16, pool_w16 = (
        w.astype(BF16) for w in (w_in, w_ret, w_pool, w_att, w_out, pool_w))
    x2 = x.reshape(batch * seq, d)
    h = _norm(x2, norm_g[0][None, :])
    for l in range(depth):
        qk_gains = jnp.concatenate(
            [jnp.tile(attn_q_gain[l], ATT_Q_HEADS), jnp.tile(attn_k_gain[l], ATT_KV_HEADS),
             jnp.ones((ATT_KV_WIDTH,), F32)])[None, :]
        za = _proj_att(h, w_in16, l, qk_gains, acos, asin, seq)
        zm = _proj_main(h, w_in16, l)
        ya = _retention(zm, ret_decay_fwd[l], ret_decay_bwd[l], rcos, rsin, batch, seq)
        yb = _pool(zm, pool_w16, pool_scale[l][None, :], l, batch, seq)
        yc = _attention(za, zm, attn_sink[l], batch, seq)
        g_next = norm_g[l + 1][None, :] if l + 1 < depth else None
        res = _merge(x2, ya, yb, yc, zm, w_ret16, w_pool16, w_att16, w_out16, l, g_next)
        x2, h = res if g_next is not None else (res, None)
    return x2.reshape(batch, seq, d)
```

```python
import functools

import jax
import jax.numpy as jnp
from jax import lax
from jax.experimental import pallas as pl
from jax.experimental.pallas import tpu as pltpu

F32 = jnp.float32
BF16 = jnp.bfloat16

D_MODEL = 2048
BRANCH = D_MODEL // 2
RET_HEADS = 4
RET_DH = BRANCH // RET_HEADS
RET_CHUNK = 256
RET_ROPE_BASE = 10000.0
POOL_WINDOWS = (2, 4, 8, 16)
POOL_DG = BRANCH // len(POOL_WINDOWS)
POOL_MARGIN = 128
ATT_DH = 128
ATT_Q_HEADS = BRANCH // ATT_DH
ATT_KV_HEADS = ATT_Q_HEADS // 4
ATT_GROUP = ATT_Q_HEADS // ATT_KV_HEADS
ATT_KV_WIDTH = ATT_KV_HEADS * ATT_DH
ATT_BLOCK = 128
ATT_WINDOW = 128
ROPE_THETA = 500000.0
ROPE_DIMS = ATT_DH // 4
RMS_EPS = 1e-6
NEG_BIG = -1e30
N_GATES = 3
MG_WIDTH = N_GATES * D_MODEL

W_AQ = 6 * BRANCH
W_AG = W_AQ + BRANCH + 2 * ATT_KV_WIDTH
IN_WIDTH = W_AG + BRANCH + MG_WIDTH

ZM_RQ, ZM_RK, ZM_RV, ZM_RG, ZM_PV, ZM_PG, ZM_AG, ZM_MG = (k * BRANCH for k in range(8))
ZM_WIDTH = ZM_MG + MG_WIDTH
ZA_WIDTH = BRANCH + 2 * ATT_KV_WIDTH
ZA_K = BRANCH
ZA_V = BRANCH + ATT_KV_WIDTH
GATE_BLOCK = 512

VMEM_LIMIT = 56 * 1024 * 1024


def _silu(x):
    return x * jax.nn.sigmoid(x)


def _rms_norm_rows(x, gain):
    ms = jnp.mean(x * x, axis=-1, keepdims=True)
    return x * lax.rsqrt(ms + RMS_EPS) * gain


def _norm_kernel(x_ref, g_ref, h_ref):
    h_ref[...] = _rms_norm_rows(x_ref[...], g_ref[...]).astype(h_ref.dtype)


def _norm(x2, g, tm=512):
    m, d = x2.shape
    return pl.pallas_call(
        _norm_kernel,
        grid=(m // tm,),
        in_specs=[pl.BlockSpec((tm, d), lambda i: (i, 0)),
                  pl.BlockSpec((1, d), lambda i: (0, 0))],
        out_specs=pl.BlockSpec((tm, d), lambda i: (i, 0)),
        out_shape=jax.ShapeDtypeStruct((m, d), BF16),
        compiler_params=pltpu.CompilerParams(
            dimension_semantics=("arbitrary",), vmem_limit_bytes=VMEM_LIMIT),
        name="norm",
    )(x2, g)


_GRID2 = pltpu.CompilerParams(dimension_semantics=("arbitrary", "arbitrary"),
                              vmem_limit_bytes=VMEM_LIMIT)


def _proj_main_kernel(h_ref, *refs):
    w_refs, z_ref = refs[:-1], refs[-1]
    tn = w_refs[0].shape[1]
    h = h_ref[...]
    for k, w_ref in enumerate(w_refs):
        z_ref[:, k * tn:(k + 1) * tn] = jnp.dot(
            h, w_ref[...], preferred_element_type=F32).astype(z_ref.dtype)


def _proj_main(h, w_all, layer, tm=2048, tn=512, tiles_per_step=2):
    m, d = h.shape
    n_before = W_AQ // tn
    skip = (W_AG - W_AQ) // tn
    n_tiles = ZM_WIDTH // tn
    assert n_tiles % tiles_per_step == 0

    def w_spec(k):
        def index(i, j):
            tile = j * tiles_per_step + k
            return layer, 0, jnp.where(tile < n_before, tile, tile + skip)
        return pl.BlockSpec((None, d, tn), index)

    return pl.pallas_call(
        _proj_main_kernel,
        grid=(m // tm, n_tiles // tiles_per_step),
        in_specs=[pl.BlockSpec((tm, d), lambda i, j: (i, 0))]
                 + [w_spec(k) for k in range(tiles_per_step)],
        out_specs=pl.BlockSpec((tm, tiles_per_step * tn), lambda i, j: (i, j)),
        out_shape=jax.ShapeDtypeStruct((m, ZM_WIDTH), BF16),
        compiler_params=_GRID2,
        name="proj_main",
    )(h, *([w_all] * tiles_per_step))


def _rope_partial(x, cos, sin_signed):
    lane = lax.broadcasted_iota(jnp.int32, x.shape, 1)
    up = pltpu.roll(x, ATT_DH - ROPE_DIMS // 2, 1)
    down = pltpu.roll(x, ROPE_DIMS // 2, 1)
    partner = jnp.where(lane < ROPE_DIMS // 2, up, down)
    return x * cos + partner * sin_signed


def _proj_att_kernel(h_ref, w_ref, gain_ref, cos_ref, sin_ref, z_ref, acc_s, *,
                     n_col, heads_per_tile, n_qk_heads):
    t = pl.program_id(0)

    @pl.when(t == 0)
    def _():
        acc_s[...] = jnp.zeros_like(acc_s)

    first_head = (jnp.maximum(t - 1, 0) % n_col) * heads_per_tile
    prev = acc_s[...]
    acc_s[...] = jnp.dot(h_ref[...], w_ref[...], preferred_element_type=F32)
    cos = cos_ref[...]
    sin = sin_ref[...]
    outs = []
    for hh in range(heads_per_tile):
        cols = slice(hh * ATT_DH, (hh + 1) * ATT_DH)
        xh = prev[:, cols]
        qk = _rope_partial(_rms_norm_rows(xh, gain_ref[:, cols]), cos, sin)
        outs.append(jnp.where(first_head + hh < n_qk_heads, qk, xh))
    z_ref[...] = jnp.concatenate(outs, axis=1).astype(z_ref.dtype)


def _proj_att(h, w_all, layer, gains, cos, sin, seq, tm=2048, tn=512):
    m, d = h.shape
    n_col = ZA_WIDTH // tn
    n_tiles = (m // tm) * n_col
    pos_tiles = seq // tm

    def cur(t):
        tt = jnp.minimum(t, n_tiles - 1)
        return tt // n_col, tt % n_col

    def prev(t):
        tt = jnp.maximum(t - 1, 0)
        return tt // n_col, tt % n_col

    tab = pl.BlockSpec((tm, ATT_DH), lambda t: (prev(t)[0] % pos_tiles, 0))
    return pl.pallas_call(
        functools.partial(_proj_att_kernel, n_col=n_col, heads_per_tile=tn // ATT_DH,
                          n_qk_heads=ZA_V // ATT_DH),
        grid=(n_tiles + 1,),
        in_specs=[
            pl.BlockSpec((tm, d), lambda t: (cur(t)[0], 0)),
            pl.BlockSpec((None, d, tn), lambda t: (layer, 0, W_AQ // tn + cur(t)[1])),
            pl.BlockSpec((1, tn), lambda t: (0, prev(t)[1])),
            tab, tab,
        ],
        out_specs=pl.BlockSpec((tm, tn), lambda t: prev(t)),
        out_shape=jax.ShapeDtypeStruct((m, ZA_WIDTH), BF16),
        scratch_shapes=[pltpu.VMEM((tm, tn), F32)],
        compiler_params=pltpu.CompilerParams(dimension_semantics=("arbitrary",),
                                             vmem_limit_bytes=VMEM_LIMIT),
        name="proj_att",
    )(h, w_all, gains, cos, sin)


_QF, _QB, _WF, _WB = range(4)


def _ret_kernel(af_ref, ab_ref, q_ref, k_ref, v_ref, g_ref, cos_ref, sin_ref,
                o_ref, qx_s, kx_s, acc_s, sf_s, sb_s, dm_s, tab_s, *, seq):
    C = RET_CHUNK
    n_chunks = seq // C
    half = RET_DH // 2
    h = pl.program_id(1)

    lg_f = -jnp.exp(jnp.full((C, half), af_ref[h], F32))
    lg_b = -jnp.exp(jnp.full((C, half), ab_ref[h], F32))
    pos = lax.broadcasted_iota(jnp.int32, (C, half), 0).astype(F32)
    tab_s[_QF] = jnp.exp(lg_f * (pos + 1.0))
    tab_s[_QB] = jnp.exp(lg_b * (C - pos))
    tab_s[_WF] = jnp.exp(lg_f * (C - 1.0 - pos))
    tab_s[_WB] = jnp.exp(lg_b * pos)
    dec_f = jnp.exp(lg_f[:1] * C)
    dec_b = jnp.exp(lg_b[:1] * C)
    dec_f = jnp.concatenate([dec_f, dec_f], axis=1)
    dec_b = jnp.concatenate([dec_b, dec_b], axis=1)
    for c in range(C // half):
        row = lax.broadcasted_iota(jnp.int32, (C, half), 0)
        col = lax.broadcasted_iota(jnp.int32, (C, half), 1) + c * half
        lag = (row - col).astype(F32)
        alag = jnp.abs(lag)
        dm_s[:, c * half:(c + 1) * half] = jnp.where(
            lag >= 0, jnp.exp(lg_f * alag), jnp.exp(lg_b * alag))
    sf_s[...] = jnp.zeros_like(sf_s)
    sb_s[...] = jnp.zeros_like(sb_s)

    def rot(x, cos, sin):
        x1, x2 = x[:, :half], x[:, half:]
        return x1 * cos - x2 * sin, x2 * cos + x1 * sin

    def scaled16(x12, t):
        tab = tab_s[t]
        return jnp.concatenate([x12[0] * tab, x12[1] * tab], axis=1).astype(BF16)

    def update_state(s_ref, dec, k16, v):
        kv = lax.dot_general(k16, v, (((0,), (0,)), ((), ())), preferred_element_type=F32)
        s_ref[...] = s_ref[...] * dec + kv

    def first_visit(n, s_ref, dec, t_q, t_k, t_q_later, t_k_later):
        r = pl.ds(pl.multiple_of(n * C, C), C)
        cos = cos_ref[r, :]
        sin = sin_ref[r, :]
        q12 = rot(q_ref[r, :].astype(F32), cos, sin)
        k12 = rot(k_ref[r, :].astype(F32), cos, sin)
        k12 = (k12[0] * (RET_DH ** -0.5), k12[1] * (RET_DH ** -0.5))
        v = v_ref[r, :]
        q16 = jnp.concatenate(q12, axis=1).astype(BF16)
        k16 = jnp.concatenate(k12, axis=1).astype(BF16)
        scores = lax.dot_general(q16, k16, (((1,), (1,)), ((), ())),
                                 preferred_element_type=F32) * dm_s[...]
        o = jnp.dot(scores.astype(BF16), v, preferred_element_type=F32)
        o = o + jnp.dot(scaled16(q12, t_q), s_ref[...].astype(BF16),
                        preferred_element_type=F32)
        acc_s[r, :] = o
        update_state(s_ref, dec, scaled16(k12, t_k), v)
        qx_s[r, :] = scaled16(q12, t_q_later)
        kx_s[r, :] = scaled16(k12, t_k_later)

    def second_visit(n, s_ref, dec):
        r = pl.ds(pl.multiple_of(n * C, C), C)
        o = acc_s[r, :] + jnp.dot(qx_s[r, :], s_ref[...].astype(BF16),
                                  preferred_element_type=F32)
        y = o * lax.rsqrt(jnp.mean(o * o, axis=-1, keepdims=True) + RMS_EPS)
        o_ref[r, :] = (y * _silu(g_ref[r, :].astype(F32))).astype(o_ref.dtype)
        update_state(s_ref, dec, kx_s[r, :], v_ref[r, :])

    def first_half(i, carry):
        first_visit(i, sf_s, dec_f, _QF, _WF, _QB, _WB)
        first_visit(n_chunks - 1 - i, sb_s, dec_b, _QB, _WB, _QF, _WF)
        return carry

    def second_half(i, carry):
        second_visit(i, sf_s, dec_f)
        second_visit(n_chunks - 1 - i, sb_s, dec_b)
        return carry

    lax.fori_loop(0, n_chunks // 2, first_half, 0, unroll=4)
    lax.fori_loop(n_chunks // 2, n_chunks, second_half, 0, unroll=4)


def _retention(zm, a_f, a_b, cos, sin, batch, seq):
    dh = RET_DH
    assert seq % (2 * RET_CHUNK) == 0
    blk = lambda off: pl.BlockSpec((seq, dh), lambda b, h, o=off // dh: (b, o + h))
    smem = pl.BlockSpec(memory_space=pltpu.SMEM)
    tab = pl.BlockSpec((seq, dh // 2), lambda b, h: (0, 0), pipeline_mode=pl.Buffered(1))
    return pl.pallas_call(
        functools.partial(_ret_kernel, seq=seq),
        grid=(batch, RET_HEADS),
        in_specs=[smem, smem, blk(ZM_RQ), blk(ZM_RK), blk(ZM_RV), blk(ZM_RG), tab, tab],
        out_specs=pl.BlockSpec((seq, dh), lambda b, h: (b, h)),
        out_shape=jax.ShapeDtypeStruct((batch * seq, BRANCH), BF16),
        scratch_shapes=[
            pltpu.VMEM((seq, dh), BF16),
            pltpu.VMEM((seq, dh), BF16),
            pltpu.VMEM((seq, dh), F32),
            pltpu.VMEM((dh, dh), F32),
            pltpu.VMEM((dh, dh), F32),
            pltpu.VMEM((RET_CHUNK, RET_CHUNK), F32),
            pltpu.VMEM((4, RET_CHUNK, dh // 2), F32),
        ],
        compiler_params=_GRID2,
        name="retention",
    )(a_f, a_b, zm, zm, zm, zm, cos, sin)


def _pool_kernel(hw_ref, u_ref, g_ref, cnt_ref, w_ref, sc_ref, o_ref, pad_s, band_s, *, seq, tile):
    M = POOL_MARGIN
    K = tile + 2 * M
    pair = 2 * tile
    hw = hw_ref[pl.program_id(1)]

    pad_s[0:M, :] = jnp.zeros((M, POOL_DG), BF16)
    pad_s[M + seq:M + seq + M, :] = jnp.zeros((M, POOL_DG), BF16)

    def fill(t, carry):
        r = pl.multiple_of(t * pair, pair)
        pad_s[pl.ds(M + r, pair), :] = u_ref[pl.ds(r, pair), :]
        return carry

    lax.fori_loop(0, seq // pair, fill, 0)

    d = (lax.broadcasted_iota(jnp.int32, (tile, K), 1)
         - lax.broadcasted_iota(jnp.int32, (tile, K), 0) - M)
    band_s[...] = jnp.where(d >= -hw, jnp.where(d < hw, 1.0, 0.0), 0.0).astype(BF16)
    w = w_ref[0]
    scale = sc_ref[...]

    def body(t, carry):
        r = pl.multiple_of(t * pair, pair)
        rows = pl.ds(r, pair)
        windows = jnp.concatenate([pad_s[pl.ds(r, K), :], pad_s[pl.ds(r + tile, K), :]], axis=1)
        wsum = jnp.dot(band_s[...], windows, preferred_element_type=F32)
        wsum = jnp.concatenate([wsum[:, :POOL_DG], wsum[:, POOL_DG:]], axis=0)
        cnt = cnt_ref[rows, :]
        cnt = jnp.concatenate([cnt] * (POOL_DG // cnt.shape[1]), axis=1)
        p = wsum / cnt - u_ref[rows, :].astype(F32)
        y = jnp.dot(p.astype(BF16), w, preferred_element_type=F32) * scale
        gate = _silu(g_ref[rows, :].astype(F32))
        o_ref[rows, :] = (y * gate).astype(o_ref.dtype)
        return carry

    lax.fori_loop(0, seq // pair, body, 0, unroll=4)


def _pool(zm, pool_w16, pool_scale, layer, batch, seq, tile=256):
    dg = POOL_DG
    n_groups = len(POOL_WINDOWS)
    assert max(POOL_WINDOWS) // 2 <= POOL_MARGIN and seq >= tile + 2 * POOL_MARGIN
    half_windows = jnp.asarray([w // 2 for w in POOL_WINDOWS], jnp.int32)
    pos = jnp.arange(seq, dtype=jnp.int32)[None, :, None]
    hw3 = half_windows[:, None, None]
    counts = (jnp.minimum(pos + hw3, seq) - jnp.maximum(pos - hw3, 0)).astype(F32)
    counts = jnp.broadcast_to(counts, (n_groups, seq, 128))
    return pl.pallas_call(
        functools.partial(_pool_kernel, seq=seq, tile=tile),
        grid=(batch, n_groups),
        in_specs=[
            pl.BlockSpec(memory_space=pltpu.SMEM),
            pl.BlockSpec((seq, dg), lambda b, g: (b, ZM_PV // dg + g)),
            pl.BlockSpec((seq, dg), lambda b, g: (b, ZM_PG // dg + g)),
            pl.BlockSpec((None, seq, 128), lambda b, g: (g, 0, 0)),
            pl.BlockSpec((None, 1, dg, dg), lambda b, g: (layer, g, 0, 0)),
            pl.BlockSpec((1, dg), lambda b, g: (0, g)),
        ],
        out_specs=pl.BlockSpec((seq, dg), lambda b, g: (b, g)),
        out_shape=jax.ShapeDtypeStruct((batch * seq, BRANCH), BF16),
        scratch_shapes=[pltpu.VMEM((seq + 2 * POOL_MARGIN, dg), BF16),
                        pltpu.VMEM((tile, tile + 2 * POOL_MARGIN), BF16)],
        compiler_params=_GRID2,
        name="pool",
    )(half_windows, zm, zm, counts, pool_w16, pool_scale)


def _att_kernel(sink_ref, q_ref, k_ref, v_ref, g_ref, o_ref, *, seq, nq):
    L = ATT_BLOCK
    G = ATT_GROUP
    kk = pl.program_id(1)
    n = pl.program_id(2)

    scale = ATT_DH ** -0.5
    exp2_coef = scale * 1.4426950408889634
    row = lax.broadcasted_iota(jnp.int32, (L, 3 * L), 0)
    col = lax.broadcasted_iota(jnp.int32, (L, 3 * L), 1)
    for blk in range(nq):
        nb = n * nq + blk
        rows = slice(blk * L, (blk + 1) * L)
        q = jnp.concatenate([q_ref[rows, h * ATT_DH:(h + 1) * ATT_DH] for h in range(G)],
                            axis=0)
        start = jnp.clip((nb - 1) * L, 0, seq - 3 * L)
        start = pl.multiple_of(start, L)
        kw = k_ref[pl.ds(start, 3 * L), :]
        vw = v_ref[pl.ds(start, 3 * L), :]
        s = lax.dot_general(q, kw, (((1,), (1,)), ((), ())),
                            preferred_element_type=F32)
        valid = jnp.abs(col - row + (start - nb * L)) <= ATT_WINDOW
        outs = []
        for h in range(G):
            sh = jnp.where(valid, s[h * L:(h + 1) * L, :], NEG_BIG)
            sk = jnp.full((L, 1), sink_ref[kk * G + h] * (1.0 / scale), F32)
            m = jnp.maximum(jnp.max(sh, axis=-1, keepdims=True), sk)
            p = jnp.exp2((sh - m) * exp2_coef)
            denom = jnp.sum(p, axis=-1, keepdims=True) + jnp.exp2((sk - m) * exp2_coef)
            oh = jnp.dot(p.astype(BF16), vw, preferred_element_type=F32) / denom
            gate = _silu(g_ref[rows, h * ATT_DH:(h + 1) * ATT_DH].astype(F32))
            outs.append((oh * gate).astype(o_ref.dtype))
        o_ref[rows, :] = jnp.concatenate(outs, axis=1)


def _attention(za, zm, sink, batch, seq, nq=8):
    L = ATT_BLOCK
    n_blocks = seq // L
    assert n_blocks % nq == 0
    n_steps = n_blocks // nq
    gw = ATT_GROUP * ATT_DH
    smem = pl.BlockSpec(memory_space=pltpu.SMEM)
    return pl.pallas_call(
        functools.partial(_att_kernel, seq=seq, nq=nq),
        grid=(batch, ATT_KV_HEADS, n_steps),
        in_specs=[
            smem,
            pl.BlockSpec((nq * L, gw), lambda b, k, n: (b * n_steps + n, k)),
            pl.BlockSpec((seq, ATT_DH), lambda b, k, n: (b, ZA_K // ATT_DH + k)),
            pl.BlockSpec((seq, ATT_DH), lambda b, k, n: (b, ZA_V // ATT_DH + k)),
            pl.BlockSpec((nq * L, gw), lambda b, k, n: (b * n_steps + n, ZM_AG // gw + k)),
        ],
        out_specs=pl.BlockSpec((nq * L, gw), lambda b, k, n: (b * n_steps + n, k)),
        out_shape=jax.ShapeDtypeStruct((batch * seq, BRANCH), BF16),
        compiler_params=pltpu.CompilerParams(
            dimension_semantics=("arbitrary", "arbitrary", "arbitrary"),
            vmem_limit_bytes=VMEM_LIMIT),
        name="attention",
    )(sink, za, za, za, zm)


def _merge_kernel(x_ref, ya_ref, yb_ref, yc_ref, *rest, n_col, emit_next):
    gate_refs = rest[:N_GATES * n_col]
    rest = rest[N_GATES * n_col:]
    if emit_next:
        wa_ref, wb_ref, wc_ref, wo_ref, gn_ref, o_ref, hn_ref, m_s = rest
    else:
        wa_ref, wb_ref, wc_ref, wo_ref, o_ref, m_s = rest
    tc = GATE_BLOCK
    ya = ya_ref[...]
    yb = yb_ref[...]
    yc = yc_ref[...]
    for c in range(n_col):
        cs = slice(c * tc, (c + 1) * tc)
        a = jnp.dot(ya, wa_ref[:, cs], preferred_element_type=F32)
        b = jnp.dot(yb, wb_ref[:, cs], preferred_element_type=F32)
        cc = jnp.dot(yc, wc_ref[:, cs], preferred_element_type=F32)
        g0, g1, g2 = (jax.nn.sigmoid(gate_refs[j * n_col + c][...].astype(F32))
                      for j in range(N_GATES))
        m_s[:, cs] = (g0 * a + g1 * b + g2 * cc).astype(BF16)
    x_new = x_ref[...] + jnp.dot(m_s[...], wo_ref[...], preferred_element_type=F32)
    o_ref[...] = x_new
    if emit_next:
        hn_ref[...] = _rms_norm_rows(x_new, gn_ref[...]).astype(hn_ref.dtype)


def _merge(x2, ya, yb, yc, zm, wa, wb, wc, wo, layer, g_next, tm=256):
    m, d = x2.shape
    n_col = d // GATE_BLOCK
    emit_next = g_next is not None
    row = lambda w: pl.BlockSpec((tm, w), lambda i: (i, 0))
    gate = lambda j, c: pl.BlockSpec(
        (tm, GATE_BLOCK), lambda i, o=ZM_MG // GATE_BLOCK + j * n_col + c: (i, o))
    gates = [gate(j, c) for j in range(N_GATES) for c in range(n_col)]
    wspec = lambda k: pl.BlockSpec((None, k, d), lambda i: (layer, 0, 0),
                                   pipeline_mode=pl.Buffered(1))
    in_specs = ([row(d), row(BRANCH), row(BRANCH), row(BRANCH)] + gates
                + [wspec(BRANCH), wspec(BRANCH), wspec(BRANCH), wspec(d)])
    args = [x2, ya, yb, yc] + [zm] * len(gates) + [wa, wb, wc, wo]
    out_specs = row(d)
    out_shape = jax.ShapeDtypeStruct((m, d), F32)
    if emit_next:
        in_specs.append(pl.BlockSpec((1, d), lambda i: (0, 0)))
        args.append(g_next)
        out_specs = (out_specs, row(d))
        out_shape = (out_shape, jax.ShapeDtypeStruct((m, d), BF16))
    return pl.pallas_call(
        functools.partial(_merge_kernel, n_col=n_col, emit_next=emit_next),
        grid=(m // tm,),
        in_specs=in_specs,
        out_specs=out_specs,
        out_shape=out_shape,
        scratch_shapes=[pltpu.VMEM((tm, d), BF16)],
        compiler_params=pltpu.CompilerParams(
            dimension_semantics=("arbitrary",),
            vmem_limit_bytes=VMEM_LIMIT),
        name="merge",
    )(*args)


def _ret_tables(seq):
    half = RET_DH // 2
    inv = 1.0 / (RET_ROPE_BASE ** jnp.linspace(0.0, 1.0, half, dtype=F32))
    ang = jnp.arange(seq, dtype=F32)[:, None] * inv[None, :]
    return jnp.cos(ang), jnp.sin(ang)


def _att_tables(seq):
    half = ROPE_DIMS // 2
    inv = ROPE_THETA ** (-jnp.arange(half, dtype=F32) / half)
    ang = jnp.arange(seq, dtype=F32)[:, None] * inv[None, :]
    c, s = jnp.cos(ang), jnp.sin(ang)
    rest = ATT_DH - ROPE_DIMS
    cos = jnp.concatenate([c, c, jnp.ones((seq, rest), F32)], axis=1)
    sin = jnp.concatenate([-s, s, jnp.zeros((seq, rest), F32)], axis=1)
    return cos, sin


def kernel(x, norm_g, w_in, ret_decay_fwd, ret_decay_bwd, pool_w, pool_scale,
           attn_q_gain, attn_k_gain, attn_sink, w_ret, w_pool, w_att, w_out):
    batch, seq, d = x.shape
    depth = w_in.shape[0]
    assert d == D_MODEL and w_in.shape[2] == IN_WIDTH
    assert seq >= 3 * ATT_BLOCK
    rcos, rsin = _ret_tables(seq)
    acos, asin = _att_tables(seq)
    w_in16, w_ret16, w_pool16, w_att16, w_out16, pool_w16 = (
        w.astype(BF16) for w in (w_in, w_ret, w_pool, w_att, w_out, pool_w))
    x2 = x.reshape(batch * seq, d)
    h = _norm(x2, norm_g[0][None, :])
    for l in range(depth):
        qk_gains = jnp.concatenate(
            [jnp.tile(attn_q_gain[l], ATT_Q_HEADS), jnp.tile(attn_k_gain[l], ATT_KV_HEADS),
             jnp.ones((ATT_KV_WIDTH,), F32)])[None, :]
        za = _proj_att(h, w_in16, l, qk_gains, acos, asin, seq)
        zm = _proj_main(h, w_in16, l)
        ya = _retention(zm, ret_decay_fwd[l], ret_decay_bwd[l], rcos, rsin, batch, seq)
        yb = _pool(zm, pool_w16, pool_scale[l][None, :], l, batch, seq)
        yc = _attention(za, zm, attn_sink[l], batch, seq)
        g_next = norm_g[l + 1][None, :] if l + 1 < depth else None
        res = _merge(x2, ya, yb, yc, zm, w_ret16, w_pool16, w_att16, w_out16, l, g_next)
        x2, h = res if g_next is not None else (res, None)
    return x2.reshape(batch, seq, d)
```

```python
import functools

import jax
import jax.numpy as jnp
from jax import lax
from jax.experimental import pallas as pl
from jax.experimental.pallas import tpu as pltpu

F32 = jnp.float32
BF16 = jnp.bfloat16

D_MODEL = 2048
BRANCH = D_MODEL // 2
RET_HEADS = 4
RET_DH = BRANCH // RET_HEADS
RET_CHUNK = 256
RET_ROPE_BASE = 10000.0
POOL_WINDOWS = (2, 4, 8, 16)
POOL_DG = BRANCH // len(POOL_WINDOWS)
POOL_MARGIN = 128
ATT_DH = 128
ATT_Q_HEADS = BRANCH // ATT_DH
ATT_KV_HEADS = ATT_Q_HEADS // 4
ATT_GROUP = ATT_Q_HEADS // ATT_KV_HEADS
ATT_KV_WIDTH = ATT_KV_HEADS * ATT_DH
ATT_BLOCK = 128
ATT_WINDOW = 128
ROPE_THETA = 500000.0
ROPE_DIMS = ATT_DH // 4
RMS_EPS = 1e-6
NEG_BIG = -1e30
N_GATES = 3
MG_WIDTH = N_GATES * D_MODEL

W_AQ = 6 * BRANCH
W_AG = W_AQ + BRANCH + 2 * ATT_KV_WIDTH
IN_WIDTH = W_AG + BRANCH + MG_WIDTH

ZM_RQ, ZM_RK, ZM_RV, ZM_RG, ZM_PV, ZM_PG, ZM_AG, ZM_MG = (k * BRANCH for k in range(8))
ZM_WIDTH = ZM_MG + MG_WIDTH
ZA_WIDTH = BRANCH + 2 * ATT_KV_WIDTH
ZA_K = BRANCH
ZA_V = BRANCH + ATT_KV_WIDTH
GATE_BLOCK = 512

VMEM_LIMIT = 56 * 1024 * 1024


def _silu(x):
    return x * jax.nn.sigmoid(x)


def _rms_norm_rows(x, gain):
    ms = jnp.mean(x * x, axis=-1, keepdims=True)
    return x * lax.rsqrt(ms + RMS_EPS) * gain


def _norm_kernel(x_ref, g_ref, h_ref):
    h_ref[...] = _rms_norm_rows(x_ref[...], g_ref[...]).astype(h_ref.dtype)


def _norm(x2, g, tm=512):
    m, d = x2.shape
    return pl.pallas_call(
        _norm_kernel,
        grid=(m // tm,),
        in_specs=[pl.BlockSpec((tm, d), lambda i: (i, 0)),
                  pl.BlockSpec((1, d), lambda i: (0, 0))],
        out_specs=pl.BlockSpec((tm, d), lambda i: (i, 0)),
        out_shape=jax.ShapeDtypeStruct((m, d), BF16),
        compiler_params=pltpu.CompilerParams(
            dimension_semantics=("arbitrary",), vmem_limit_bytes=VMEM_LIMIT),
        name="norm",
    )(x2, g)


_GRID2 = pltpu.CompilerParams(dimension_semantics=("arbitrary", "arbitrary"),
                              vmem_limit_bytes=VMEM_LIMIT)


def _proj_main_kernel(h_ref, *refs):
    w_refs, z_ref, w16_s = refs[:-2], refs[-2], refs[-1]
    tn = w_refs[0].shape[1]

    @pl.when(pl.program_id(1) == 0)
    def _():
        for k, w_ref in enumerate(w_refs):
            w16_s[k] = w_ref[...].astype(BF16)

    h = h_ref[...]
    for k in range(len(w_refs)):
        z_ref[:, k * tn:(k + 1) * tn] = jnp.dot(
            h, w16_s[k], preferred_element_type=F32).astype(z_ref.dtype)


def _proj_main(h, w_all, layer, tm=2048, tn=512, tiles_per_step=2):
    m, d = h.shape
    n_before = W_AQ // tn
    skip = (W_AG - W_AQ) // tn
    n_tiles = ZM_WIDTH // tn
    assert n_tiles % tiles_per_step == 0

    def w_spec(k):
        def index(j, i):
            tile = j * tiles_per_step + k
            return layer, 0, jnp.where(tile < n_before, tile, tile + skip)
        return pl.BlockSpec((None, d, tn), index)

    return pl.pallas_call(
        _proj_main_kernel,
        grid=(n_tiles // tiles_per_step, m // tm),
        in_specs=[pl.BlockSpec((tm, d), lambda j, i: (i, 0))]
                 + [w_spec(k) for k in range(tiles_per_step)],
        out_specs=pl.BlockSpec((tm, tiles_per_step * tn), lambda j, i: (i, j)),
        out_shape=jax.ShapeDtypeStruct((m, ZM_WIDTH), BF16),
        scratch_shapes=[pltpu.VMEM((tiles_per_step, d, tn), BF16)],
        compiler_params=_GRID2,
        name="proj_main",
    )(h, *([w_all] * tiles_per_step))


def _rope_partial(x, cos, sin_signed):
    lane = lax.broadcasted_iota(jnp.int32, x.shape, 1)
    up = pltpu.roll(x, ATT_DH - ROPE_DIMS // 2, 1)
    down = pltpu.roll(x, ROPE_DIMS // 2, 1)
    partner = jnp.where(lane < ROPE_DIMS // 2, up, down)
    return x * cos + partner * sin_signed


def _proj_att_kernel(h_ref, w_ref, gain_ref, cos_ref, sin_ref, z_ref, acc_s, w16_s, *,
                     n_row, heads_per_tile, n_qk_heads):
    t = pl.program_id(0)

    @pl.when(t == 0)
    def _():
        acc_s[...] = jnp.zeros_like(acc_s)

    @pl.when(t % n_row == 0)
    def _():
        w16_s[...] = w_ref[...].astype(BF16)

    first_head = (jnp.maximum(t - 1, 0) // n_row) * heads_per_tile
    prev = acc_s[...]
    acc_s[...] = jnp.dot(h_ref[...], w16_s[...], preferred_element_type=F32)
    cos = cos_ref[...]
    sin = sin_ref[...]
    outs = []
    for hh in range(heads_per_tile):
        cols = slice(hh * ATT_DH, (hh + 1) * ATT_DH)
        xh = prev[:, cols]
        qk = _rope_partial(_rms_norm_rows(xh, gain_ref[:, cols]), cos, sin)
        outs.append(jnp.where(first_head + hh < n_qk_heads, qk, xh))
    z_ref[...] = jnp.concatenate(outs, axis=1).astype(z_ref.dtype)


def _proj_att(h, w_all, layer, gains, cos, sin, seq, tm=2048, tn=512):
    m, d = h.shape
    n_row = m // tm
    n_tiles = n_row * (ZA_WIDTH // tn)
    pos_tiles = seq // tm

    def cur(t):
        tt = jnp.minimum(t, n_tiles - 1)
        return tt % n_row, tt // n_row

    def prev(t):
        tt = jnp.maximum(t - 1, 0)
        return tt % n_row, tt // n_row

    tab = pl.BlockSpec((tm, ATT_DH), lambda t: (prev(t)[0] % pos_tiles, 0))
    return pl.pallas_call(
        functools.partial(_proj_att_kernel, n_row=n_row, heads_per_tile=tn // ATT_DH,
                          n_qk_heads=ZA_V // ATT_DH),
        grid=(n_tiles + 1,),
        in_specs=[
            pl.BlockSpec((tm, d), lambda t: (cur(t)[0], 0)),
            pl.BlockSpec((None, d, tn), lambda t: (layer, 0, W_AQ // tn + cur(t)[1])),
            pl.BlockSpec((1, tn), lambda t: (0, prev(t)[1])),
            tab, tab,
        ],
        out_specs=pl.BlockSpec((tm, tn), lambda t: prev(t)),
        out_shape=jax.ShapeDtypeStruct((m, ZA_WIDTH), BF16),
        scratch_shapes=[pltpu.VMEM((tm, tn), F32), pltpu.VMEM((d, tn), BF16)],
        compiler_params=pltpu.CompilerParams(dimension_semantics=("arbitrary",),
                                             vmem_limit_bytes=VMEM_LIMIT),
        name="proj_att",
    )(h, w_all, gains, cos, sin)


_QF, _QB, _WF, _WB = range(4)


def _ret_kernel(af_ref, ab_ref, q_ref, k_ref, v_ref, g_ref, cos_ref, sin_ref,
                o_ref, qx_s, kx_s, acc_s, sf_s, sb_s, dm_s, tab_s, *, seq):
    C = RET_CHUNK
    n_chunks = seq // C
    half = RET_DH // 2
    h = pl.program_id(1)

    lg_f = -jnp.exp(jnp.full((C, half), af_ref[h], F32))
    lg_b = -jnp.exp(jnp.full((C, half), ab_ref[h], F32))
    pos = lax.broadcasted_iota(jnp.int32, (C, half), 0).astype(F32)
    tab_s[_QF] = jnp.exp(lg_f * (pos + 1.0))
    tab_s[_QB] = jnp.exp(lg_b * (C - pos))
    tab_s[_WF] = jnp.exp(lg_f * (C - 1.0 - pos))
    tab_s[_WB] = jnp.exp(lg_b * pos)
    dec_f = jnp.exp(lg_f[:1] * C)
    dec_b = jnp.exp(lg_b[:1] * C)
    dec_f = jnp.concatenate([dec_f, dec_f], axis=1)
    dec_b = jnp.concatenate([dec_b, dec_b], axis=1)
    for c in range(C // half):
        row = lax.broadcasted_iota(jnp.int32, (C, half), 0)
        col = lax.broadcasted_iota(jnp.int32, (C, half), 1) + c * half
        lag = (row - col).astype(F32)
        alag = jnp.abs(lag)
        dm_s[:, c * half:(c + 1) * half] = jnp.where(
            lag >= 0, jnp.exp(lg_f * alag), jnp.exp(lg_b * alag))
    sf_s[...] = jnp.zeros_like(sf_s)
    sb_s[...] = jnp.zeros_like(sb_s)

    def rot(x, cos, sin):
        x1, x2 = x[:, :half], x[:, half:]
        return x1 * cos - x2 * sin, x2 * cos + x1 * sin

    def scaled16(x12, t):
        tab = tab_s[t]
        return jnp.concatenate([x12[0] * tab, x12[1] * tab], axis=1).astype(BF16)

    def update_state(s_ref, dec, k16, v):
        kv = lax.dot_general(k16, v, (((0,), (0,)), ((), ())), preferred_element_type=F32)
        s_ref[...] = s_ref[...] * dec + kv

    def first_visit(n, s_ref, dec, t_q, t_k, t_q_later, t_k_later):
        r = pl.ds(pl.multiple_of(n * C, C), C)
        cos = cos_ref[r, :]
        sin = sin_ref[r, :]
        q12 = rot(q_ref[r, :].astype(F32), cos, sin)
        k12 = rot(k_ref[r, :].astype(F32), cos, sin)
        k12 = (k12[0] * (RET_DH ** -0.5), k12[1] * (RET_DH ** -0.5))
        v = v_ref[r, :]
        q16 = jnp.concatenate(q12, axis=1).astype(BF16)
        k16 = jnp.concatenate(k12, axis=1).astype(BF16)
        scores = lax.dot_general(q16, k16, (((1,), (1,)), ((), ())),
                                 preferred_element_type=F32) * dm_s[...]
        o = jnp.dot(scores.astype(BF16), v, preferred_element_type=F32)
        o = o + jnp.dot(scaled16(q12, t_q), s_ref[...].astype(BF16),
                        preferred_element_type=F32)
        acc_s[r, :] = o
        update_state(s_ref, dec, scaled16(k12, t_k), v)
        qx_s[r, :] = scaled16(q12, t_q_later)
        kx_s[r, :] = scaled16(k12, t_k_later)

    def second_visit(n, s_ref, dec):
        r = pl.ds(pl.multiple_of(n * C, C), C)
        o = acc_s[r, :] + jnp.dot(qx_s[r, :], s_ref[...].astype(BF16),
                                  preferred_element_type=F32)
        y = o * lax.rsqrt(jnp.mean(o * o, axis=-1, keepdims=True) + RMS_EPS)
        o_ref[r, :] = (y * _silu(g_ref[r, :].astype(F32))).astype(o_ref.dtype)
        update_state(s_ref, dec, kx_s[r, :], v_ref[r, :])

    def first_half(i, carry):
        first_visit(i, sf_s, dec_f, _QF, _WF, _QB, _WB)
        first_visit(n_chunks - 1 - i, sb_s, dec_b, _QB, _WB, _QF, _WF)
        return carry

    def second_half(i, carry):
        second_visit(i, sf_s, dec_f)
        second_visit(n_chunks - 1 - i, sb_s, dec_b)
        return carry

    lax.fori_loop(0, n_chunks // 2, first_half, 0, unroll=4)
    lax.fori_loop(n_chunks // 2, n_chunks, second_half, 0, unroll=4)


def _retention(zm, a_f, a_b, cos, sin, batch, seq):
    dh = RET_DH
    assert seq % (2 * RET_CHUNK) == 0
    blk = lambda off: pl.BlockSpec((seq, dh), lambda b, h, o=off // dh: (b, o + h))
    smem = pl.BlockSpec(memory_space=pltpu.SMEM)
    tab = pl.BlockSpec((seq, dh // 2), lambda b, h: (0, 0), pipeline_mode=pl.Buffered(1))
    return pl.pallas_call(
        functools.partial(_ret_kernel, seq=seq),
        grid=(batch, RET_HEADS),
        in_specs=[smem, smem, blk(ZM_RQ), blk(ZM_RK), blk(ZM_RV), blk(ZM_RG), tab, tab],
        out_specs=pl.BlockSpec((seq, dh), lambda b, h: (b, h)),
        out_shape=jax.ShapeDtypeStruct((batch * seq, BRANCH), BF16),
        scratch_shapes=[
            pltpu.VMEM((seq, dh), BF16),
            pltpu.VMEM((seq, dh), BF16),
            pltpu.VMEM((seq, dh), F32),
            pltpu.VMEM((dh, dh), F32),
            pltpu.VMEM((dh, dh), F32),
            pltpu.VMEM((RET_CHUNK, RET_CHUNK), F32),
            pltpu.VMEM((4, RET_CHUNK, dh // 2), F32),
        ],
        compiler_params=_GRID2,
        name="retention",
    )(a_f, a_b, zm, zm, zm, zm, cos, sin)


def _pool_kernel(hw_ref, u_ref, g_ref, cnt_ref, w_ref, sc_ref, o_ref, pad_s, band_s, *, seq, tile):
    M = POOL_MARGIN
    K = tile + 2 * M
    pair = 2 * tile
    hw = hw_ref[pl.program_id(1)]

    pad_s[0:M, :] = jnp.zeros((M, POOL_DG), BF16)
    pad_s[M + seq:M + seq + M, :] = jnp.zeros((M, POOL_DG), BF16)

    def fill(t, carry):
        r = pl.multiple_of(t * pair, pair)
        pad_s[pl.ds(M + r, pair), :] = u_ref[pl.ds(r, pair), :]
        return carry

    lax.fori_loop(0, seq // pair, fill, 0)

    d = (lax.broadcasted_iota(jnp.int32, (tile, K), 1)
         - lax.broadcasted_iota(jnp.int32, (tile, K), 0) - M)
    band_s[...] = jnp.where(d >= -hw, jnp.where(d < hw, 1.0, 0.0), 0.0).astype(BF16)
    w = w_ref[0]
    scale = sc_ref[...]

    def body(t, carry):
        r = pl.multiple_of(t * pair, pair)
        rows = pl.ds(r, pair)
        windows = jnp.concatenate([pad_s[pl.ds(r, K), :], pad_s[pl.ds(r + tile, K), :]], axis=1)
        wsum = jnp.dot(band_s[...], windows, preferred_element_type=F32)
        wsum = jnp.concatenate([wsum[:, :POOL_DG], wsum[:, POOL_DG:]], axis=0)
        cnt = cnt_ref[rows, :]
        cnt = jnp.concatenate([cnt] * (POOL_DG // cnt.shape[1]), axis=1)
        p = wsum / cnt - u_ref[rows, :].astype(F32)
        y = jnp.dot(p.astype(BF16), w, preferred_element_type=F32) * scale
        gate = _silu(g_ref[rows, :].astype(F32))
        o_ref[rows, :] = (y * gate).astype(o_ref.dtype)
        return carry

    lax.fori_loop(0, seq // pair, body, 0, unroll=4)


def _pool(zm, pool_w16, pool_scale, layer, batch, seq, tile=256):
    dg = POOL_DG
    n_groups = len(POOL_WINDOWS)
    assert max(POOL_WINDOWS) // 2 <= POOL_MARGIN and seq >= tile + 2 * POOL_MARGIN
    half_windows = jnp.asarray([w // 2 for w in POOL_WINDOWS], jnp.int32)
    pos = jnp.arange(seq, dtype=jnp.int32)[None, :, None]
    hw3 = half_windows[:, None, None]
    counts = (jnp.minimum(pos + hw3, seq) - jnp.maximum(pos - hw3, 0)).astype(F32)
    counts = jnp.broadcast_to(counts, (n_groups, seq, 128))
    return pl.pallas_call(
        functools.partial(_pool_kernel, seq=seq, tile=tile),
        grid=(batch, n_groups),
        in_specs=[
            pl.BlockSpec(memory_space=pltpu.SMEM),
            pl.BlockSpec((seq, dg), lambda b, g: (b, ZM_PV // dg + g)),
            pl.BlockSpec((seq, dg), lambda b, g: (b, ZM_PG // dg + g)),
            pl.BlockSpec((None, seq, 128), lambda b, g: (g, 0, 0)),
            pl.BlockSpec((None, 1, dg, dg), lambda b, g: (layer, g, 0, 0)),
            pl.BlockSpec((1, dg), lambda b, g: (0, g)),
        ],
        out_specs=pl.BlockSpec((seq, dg), lambda b, g: (b, g)),
        out_shape=jax.ShapeDtypeStruct((batch * seq, BRANCH), BF16),
        scratch_shapes=[pltpu.VMEM((seq + 2 * POOL_MARGIN, dg), BF16),
                        pltpu.VMEM((tile, tile + 2 * POOL_MARGIN), BF16)],
        compiler_params=_GRID2,
        name="pool",
    )(half_windows, zm, zm, counts, pool_w16, pool_scale)


def _att_kernel(sink_ref, q_ref, k_ref, v_ref, g_ref, o_ref, *, seq, nq):
    L = ATT_BLOCK
    G = ATT_GROUP
    kk = pl.program_id(1)
    n = pl.program_id(2)

    scale = ATT_DH ** -0.5
    exp2_coef = scale * 1.4426950408889634
    row = lax.broadcasted_iota(jnp.int32, (L, 3 * L), 0)
    col = lax.broadcasted_iota(jnp.int32, (L, 3 * L), 1)
    for blk in range(nq):
        nb = n * nq + blk
        rows = slice(blk * L, (blk + 1) * L)
        q = jnp.concatenate([q_ref[rows, h * ATT_DH:(h + 1) * ATT_DH] for h in range(G)],
                            axis=0)
        start = jnp.clip((nb - 1) * L, 0, seq - 3 * L)
        start = pl.multiple_of(start, L)
        kw = k_ref[pl.ds(start, 3 * L), :]
        vw = v_ref[pl.ds(start, 3 * L), :]
        s = lax.dot_general(q, kw, (((1,), (1,)), ((), ())),
                            preferred_element_type=F32)
        valid = jnp.abs(col - row + (start - nb * L)) <= ATT_WINDOW
        outs = []
        for h in range(G):
            sh = jnp.where(valid, s[h * L:(h + 1) * L, :], NEG_BIG)
            sk = jnp.full((L, 1), sink_ref[kk * G + h] * (1.0 / scale), F32)
            m = jnp.maximum(jnp.max(sh, axis=-1, keepdims=True), sk)
            p = jnp.exp2((sh - m) * exp2_coef)
            denom = jnp.sum(p, axis=-1, keepdims=True) + jnp.exp2((sk - m) * exp2_coef)
            oh = jnp.dot(p.astype(BF16), vw, preferred_element_type=F32) / denom
            gate = _silu(g_ref[rows, h * ATT_DH:(h + 1) * ATT_DH].astype(F32))
            outs.append((oh * gate).astype(o_ref.dtype))
        o_ref[rows, :] = jnp.concatenate(outs, axis=1)


def _attention(za, zm, sink, batch, seq, nq=8):
    L = ATT_BLOCK
    n_blocks = seq // L
    assert n_blocks % nq == 0
    n_steps = n_blocks // nq
    gw = ATT_GROUP * ATT_DH
    smem = pl.BlockSpec(memory_space=pltpu.SMEM)
    return pl.pallas_call(
        functools.partial(_att_kernel, seq=seq, nq=nq),
        grid=(batch, ATT_KV_HEADS, n_steps),
        in_specs=[
            smem,
            pl.BlockSpec((nq * L, gw), lambda b, k, n: (b * n_steps + n, k)),
            pl.BlockSpec((seq, ATT_DH), lambda b, k, n: (b, ZA_K // ATT_DH + k)),
            pl.BlockSpec((seq, ATT_DH), lambda b, k, n: (b, ZA_V // ATT_DH + k)),
            pl.BlockSpec((nq * L, gw), lambda b, k, n: (b * n_steps + n, ZM_AG // gw + k)),
        ],
        out_specs=pl.BlockSpec((nq * L, gw), lambda b, k, n: (b * n_steps + n, k)),
        out_shape=jax.ShapeDtypeStruct((batch * seq, BRANCH), BF16),
        compiler_params=pltpu.CompilerParams(
            dimension_semantics=("arbitrary", "arbitrary", "arbitrary"),
            vmem_limit_bytes=VMEM_LIMIT),
        name="attention",
    )(sink, za, za, za, zm)


def _merge_kernel(x_ref, ya_ref, yb_ref, yc_ref, *rest, n_col, emit_next):
    gate_refs = rest[:N_GATES * n_col]
    rest = rest[N_GATES * n_col:]
    if emit_next:
        wa_ref, wb_ref, wc_ref, wo_ref, gn_ref, o_ref, hn_ref, m_s = rest
    else:
        wa_ref, wb_ref, wc_ref, wo_ref, o_ref, m_s = rest
    tc = GATE_BLOCK
    ya = ya_ref[...]
    yb = yb_ref[...]
    yc = yc_ref[...]
    for c in range(n_col):
        cs = slice(c * tc, (c + 1) * tc)
        a = jnp.dot(ya, wa_ref[:, cs], preferred_element_type=F32)
        b = jnp.dot(yb, wb_ref[:, cs], preferred_element_type=F32)
        cc = jnp.dot(yc, wc_ref[:, cs], preferred_element_type=F32)
        g0, g1, g2 = (jax.nn.sigmoid(gate_refs[j * n_col + c][...].astype(F32))
                      for j in range(N_GATES))
        m_s[:, cs] = (g0 * a + g1 * b + g2 * cc).astype(BF16)
    x_new = x_ref[...] + jnp.dot(m_s[...], wo_ref[...], preferred_element_type=F32)
    o_ref[...] = x_new
    if emit_next:
        hn_ref[...] = _rms_norm_rows(x_new, gn_ref[...]).astype(hn_ref.dtype)


def _merge(x2, ya, yb, yc, zm, wa, wb, wc, wo, layer, g_next, tm=256):
    m, d = x2.shape
    n_col = d // GATE_BLOCK
    emit_next = g_next is not None
    row = lambda w: pl.BlockSpec((tm, w), lambda i: (i, 0))
    gate = lambda j, c: pl.BlockSpec(
        (tm, GATE_BLOCK), lambda i, o=ZM_MG // GATE_BLOCK + j * n_col + c: (i, o))
    gates = [gate(j, c) for j in range(N_GATES) for c in range(n_col)]
    wspec = lambda k: pl.BlockSpec((None, k, d), lambda i: (layer, 0, 0),
                                   pipeline_mode=pl.Buffered(1))
    in_specs = ([row(d), row(BRANCH), row(BRANCH), row(BRANCH)] + gates
                + [wspec(BRANCH), wspec(BRANCH), wspec(BRANCH), wspec(d)])
    args = [x2, ya, yb, yc] + [zm] * len(gates) + [wa, wb, wc, wo]
    out_specs = row(d)
    out_shape = jax.ShapeDtypeStruct((m, d), F32)
    if emit_next:
        in_specs.append(pl.BlockSpec((1, d), lambda i: (0, 0)))
        args.append(g_next)
        out_specs = (out_specs, row(d))
        out_shape = (out_shape, jax.ShapeDtypeStruct((m, d), BF16))
    return pl.pallas_call(
        functools.partial(_merge_kernel, n_col=n_col, emit_next=emit_next),
        grid=(m // tm,),
        in_specs=in_specs,
        out_specs=out_specs,
        out_shape=out_shape,
        scratch_shapes=[pltpu.VMEM((tm, d), BF16)],
        compiler_params=pltpu.CompilerParams(
            dimension_semantics=("arbitrary",),
            vmem_limit_bytes=VMEM_LIMIT),
        name="merge",
    )(*args)


def _ret_tables(seq):
    half = RET_DH // 2
    inv = 1.0 / (RET_ROPE_BASE ** jnp.linspace(0.0, 1.0, half, dtype=F32))
    ang = jnp.arange(seq, dtype=F32)[:, None] * inv[None, :]
    return jnp.cos(ang), jnp.sin(ang)


def _att_tables(seq):
    half = ROPE_DIMS // 2
    inv = ROPE_THETA ** (-jnp.arange(half, dtype=F32) / half)
    ang = jnp.arange(seq, dtype=F32)[:, None] * inv[None, :]
    c, s = jnp.cos(ang), jnp.sin(ang)
    rest = ATT_DH - ROPE_DIMS
    cos = jnp.concatenate([c, c, jnp.ones((seq, rest), F32)], axis=1)
    sin = jnp.concatenate([-s, s, jnp.zeros((seq, rest), F32)], axis=1)
    return cos, sin


def kernel(x, norm_g, w_in, ret_decay_fwd, ret_decay_bwd, pool_w, pool_scale,
           attn_q_gain, attn_k_gain, attn_sink, w_ret, w_pool, w_att, w_out):
    batch, seq, d = x.shape
    depth = w_in.shape[0]
    assert d == D_MODEL and w_in.shape[2] == IN_WIDTH
    assert seq >= 3 * ATT_BLOCK
    rcos, rsin = _ret_tables(seq)
    acos, asin = _att_tables(seq)
    w_ret16, w_pool16, w_att16, w_out16, pool_w16 = (
        w.astype(BF16) for w in (w_ret, w_pool, w_att, w_out, pool_w))
    x2 = x.reshape(batch * seq, d)
    h = _norm(x2, norm_g[0][None, :])
    for l in range(depth):
        qk_gains = jnp.concatenate(
            [jnp.tile(attn_q_gain[l], ATT_Q_HEADS), jnp.tile(attn_k_gain[l], ATT_KV_HEADS),
             jnp.ones((ATT_KV_WIDTH,), F32)])[None, :]
        za = _proj_att(h, w_in, l, qk_gains, acos, asin, seq)
        zm = _proj_main(h, w_in, l)
        ya = _retention(zm, ret_decay_fwd[l], ret_decay_bwd[l], rcos, rsin, batch, seq)
        yb = _pool(zm, pool_w16, pool_scale[l][None, :], l, batch, seq)
        yc = _attention(za, zm, attn_sink[l], batch, seq)
        g_next = norm_g[l + 1][None, :] if l + 1 < depth else None
        res = _merge(x2, ya, yb, yc, zm, w_ret16, w_pool16, w_att16, w_out16, l, g_next)
        x2, h = res if g_next is not None else (res, None)
    return x2.reshape(batch, seq, d)
```

```python
import functools

import jax
import jax.numpy as jnp
from jax import lax
from jax.experimental import pallas as pl
from jax.experimental.pallas import tpu as pltpu

F32 = jnp.float32
BF16 = jnp.bfloat16

D_MODEL = 2048
BRANCH = D_MODEL // 2
RET_HEADS = 4
RET_DH = BRANCH // RET_HEADS
RET_CHUNK = 256
RET_ROPE_BASE = 10000.0
POOL_WINDOWS = (2, 4, 8, 16)
POOL_DG = BRANCH // len(POOL_WINDOWS)
POOL_MARGIN = 128
ATT_DH = 128
ATT_Q_HEADS = BRANCH // ATT_DH
ATT_KV_HEADS = ATT_Q_HEADS // 4
ATT_GROUP = ATT_Q_HEADS // ATT_KV_HEADS
ATT_KV_WIDTH = ATT_KV_HEADS * ATT_DH
ATT_BLOCK = 128
ATT_WINDOW = 128
ROPE_THETA = 500000.0
ROPE_DIMS = ATT_DH // 4
RMS_EPS = 1e-6
NEG_BIG = -1e30
LOG2_E = 1.4426950408889634
N_GATES = 3
MG_WIDTH = N_GATES * D_MODEL

W_AQ = 6 * BRANCH
W_AG = W_AQ + BRANCH + 2 * ATT_KV_WIDTH
IN_WIDTH = W_AG + BRANCH + MG_WIDTH

ZM_RQ, ZM_RK, ZM_RV, ZM_RG, ZM_PV, ZM_PG, ZM_AG, ZM_MG = (k * BRANCH for k in range(8))
ZM_WIDTH = ZM_MG + MG_WIDTH
ZA_WIDTH = BRANCH + 2 * ATT_KV_WIDTH
ZA_K = BRANCH
ZA_V = BRANCH + ATT_KV_WIDTH
GATE_BLOCK = 512

VMEM_LIMIT = 56 * 1024 * 1024


def _silu(x):
    return x * jax.nn.sigmoid(x)


def _rms_norm_rows(x, gain):
    ms = jnp.mean(x * x, axis=-1, keepdims=True)
    return x * lax.rsqrt(ms + RMS_EPS) * gain


def _norm_kernel(x_ref, g_ref, h_ref):
    h_ref[...] = _rms_norm_rows(x_ref[...], g_ref[...]).astype(h_ref.dtype)


def _norm(x2, g, tm=512):
    m, d = x2.shape
    return pl.pallas_call(
        _norm_kernel,
        grid=(m // tm,),
        in_specs=[pl.BlockSpec((tm, d), lambda i: (i, 0)),
                  pl.BlockSpec((1, d), lambda i: (0, 0))],
        out_specs=pl.BlockSpec((tm, d), lambda i: (i, 0)),
        out_shape=jax.ShapeDtypeStruct((m, d), BF16),
        compiler_params=pltpu.CompilerParams(
            dimension_semantics=("arbitrary",), vmem_limit_bytes=VMEM_LIMIT),
        name="norm",
    )(x2, g)


_GRID2 = pltpu.CompilerParams(dimension_semantics=("arbitrary", "arbitrary"),
                              vmem_limit_bytes=VMEM_LIMIT)


def _proj_main_kernel(h_ref, *refs):
    w_refs, z_ref, w16_s = refs[:-2], refs[-2], refs[-1]
    tn = w_refs[0].shape[1]

    @pl.when(pl.program_id(1) == 0)
    def _():
        for k, w_ref in enumerate(w_refs):
            w16_s[k] = w_ref[...].astype(BF16)

    h = h_ref[...]
    for k in range(len(w_refs)):
        z_ref[:, k * tn:(k + 1) * tn] = jnp.dot(
            h, w16_s[k], preferred_element_type=F32).astype(z_ref.dtype)


def _proj_main(h, w_all, layer, tm=2048, tn=512, tiles_per_step=2):
    m, d = h.shape
    n_before = W_AQ // tn
    skip = (W_AG - W_AQ) // tn
    n_tiles = ZM_WIDTH // tn
    assert n_tiles % tiles_per_step == 0

    def w_spec(k):
        def index(j, i):
            tile = j * tiles_per_step + k
            return layer, 0, jnp.where(tile < n_before, tile, tile + skip)
        return pl.BlockSpec((None, d, tn), index)

    return pl.pallas_call(
        _proj_main_kernel,
        grid=(n_tiles // tiles_per_step, m // tm),
        in_specs=[pl.BlockSpec((tm, d), lambda j, i: (i, 0))]
                 + [w_spec(k) for k in range(tiles_per_step)],
        out_specs=pl.BlockSpec((tm, tiles_per_step * tn), lambda j, i: (i, j)),
        out_shape=jax.ShapeDtypeStruct((m, ZM_WIDTH), BF16),
        scratch_shapes=[pltpu.VMEM((tiles_per_step, d, tn), BF16)],
        compiler_params=_GRID2,
        name="proj_main",
    )(h, *([w_all] * tiles_per_step))


def _rope_partial(x, cos, sin_signed):
    lane = lax.broadcasted_iota(jnp.int32, x.shape, 1)
    up = pltpu.roll(x, ATT_DH - ROPE_DIMS // 2, 1)
    down = pltpu.roll(x, ROPE_DIMS // 2, 1)
    partner = jnp.where(lane < ROPE_DIMS // 2, up, down)
    return x * cos + partner * sin_signed


def _proj_att_kernel(h_ref, w_ref, gain_ref, cos_ref, sin_ref, z_ref, acc_s, w16_s, *,
                     n_row, heads_per_tile, n_qk_heads):
    t = pl.program_id(0)

    @pl.when(t == 0)
    def _():
        acc_s[...] = jnp.zeros_like(acc_s)

    @pl.when(t % n_row == 0)
    def _():
        w16_s[...] = w_ref[...].astype(BF16)

    first_head = (jnp.maximum(t - 1, 0) // n_row) * heads_per_tile
    prev = acc_s[...]
    acc_s[...] = jnp.dot(h_ref[...], w16_s[...], preferred_element_type=F32)
    cos = cos_ref[...]
    sin = sin_ref[...]
    outs = []
    for hh in range(heads_per_tile):
        cols = slice(hh * ATT_DH, (hh + 1) * ATT_DH)
        xh = prev[:, cols]
        qk = _rope_partial(_rms_norm_rows(xh, gain_ref[:, cols]), cos, sin)
        outs.append(jnp.where(first_head + hh < n_qk_heads, qk, xh))
    z_ref[...] = jnp.concatenate(outs, axis=1).astype(z_ref.dtype)


def _proj_att(h, w_all, layer, gains, cos, sin, seq, tm=2048, tn=512):
    m, d = h.shape
    n_row = m // tm
    n_tiles = n_row * (ZA_WIDTH // tn)
    pos_tiles = seq // tm

    def cur(t):
        tt = jnp.minimum(t, n_tiles - 1)
        return tt % n_row, tt // n_row

    def prev(t):
        tt = jnp.maximum(t - 1, 0)
        return tt % n_row, tt // n_row

    tab = pl.BlockSpec((tm, ATT_DH), lambda t: (prev(t)[0] % pos_tiles, 0))
    return pl.pallas_call(
        functools.partial(_proj_att_kernel, n_row=n_row, heads_per_tile=tn // ATT_DH,
                          n_qk_heads=ZA_V // ATT_DH),
        grid=(n_tiles + 1,),
        in_specs=[
            pl.BlockSpec((tm, d), lambda t: (cur(t)[0], 0)),
            pl.BlockSpec((None, d, tn), lambda t: (layer, 0, W_AQ // tn + cur(t)[1])),
            pl.BlockSpec((1, tn), lambda t: (0, prev(t)[1])),
            tab, tab,
        ],
        out_specs=pl.BlockSpec((tm, tn), lambda t: prev(t)),
        out_shape=jax.ShapeDtypeStruct((m, ZA_WIDTH), BF16),
        scratch_shapes=[pltpu.VMEM((tm, tn), F32), pltpu.VMEM((d, tn), BF16)],
        compiler_params=pltpu.CompilerParams(dimension_semantics=("arbitrary",),
                                             vmem_limit_bytes=VMEM_LIMIT),
        name="proj_att",
    )(h, w_all, gains, cos, sin)


_QF, _QB, _WF, _WB = range(4)


def _ret_kernel(af_ref, ab_ref, q_ref, k_ref, v_ref, g_ref, cos_ref, sin_ref,
                o_ref, qx_s, kx_s, acc_s, sf_s, sb_s, dm_s, tab_s, *, seq):
    C = RET_CHUNK
    n_chunks = seq // C
    half = RET_DH // 2
    h = pl.program_id(1)

    lg_f = -jnp.exp(jnp.full((C, half), af_ref[h], F32))
    lg_b = -jnp.exp(jnp.full((C, half), ab_ref[h], F32))
    pos = lax.broadcasted_iota(jnp.int32, (C, half), 0).astype(F32)
    tab_s[_QF] = jnp.exp(lg_f * (pos + 1.0))
    tab_s[_QB] = jnp.exp(lg_b * (C - pos))
    k_scale = RET_DH ** -0.5
    tab_s[_WF] = jnp.exp(lg_f * (C - 1.0 - pos)) * k_scale
    tab_s[_WB] = jnp.exp(lg_b * pos) * k_scale
    dec_f = jnp.exp(lg_f[:1] * C)
    dec_b = jnp.exp(lg_b[:1] * C)
    dec_f = jnp.concatenate([dec_f, dec_f], axis=1)
    dec_b = jnp.concatenate([dec_b, dec_b], axis=1)
    for c in range(C // half):
        row = lax.broadcasted_iota(jnp.int32, (C, half), 0)
        col = lax.broadcasted_iota(jnp.int32, (C, half), 1) + c * half
        lag = (row - col).astype(F32)
        alag = jnp.abs(lag)
        dm_s[:, c * half:(c + 1) * half] = jnp.where(
            lag >= 0, jnp.exp(lg_f * alag), jnp.exp(lg_b * alag)) * k_scale
    sf_s[...] = jnp.zeros_like(sf_s)
    sb_s[...] = jnp.zeros_like(sb_s)

    def rot(x, cos, sin):
        x1, x2 = x[:, :half], x[:, half:]
        return x1 * cos - x2 * sin, x2 * cos + x1 * sin

    def scaled16(x12, t):
        tab = tab_s[t]
        return jnp.concatenate([x12[0] * tab, x12[1] * tab], axis=1).astype(BF16)

    def update_state(s_ref, dec, k16, v):
        kv = lax.dot_general(k16, v, (((0,), (0,)), ((), ())), preferred_element_type=F32)
        s_ref[...] = s_ref[...] * dec + kv

    def first_visit(n, s_ref, dec, t_q, t_k, t_q_later, t_k_later):
        r = pl.ds(pl.multiple_of(n * C, C), C)
        cos = cos_ref[r, :]
        sin = sin_ref[r, :]
        q12 = rot(q_ref[r, :].astype(F32), cos, sin)
        k12 = rot(k_ref[r, :].astype(F32), cos, sin)
        v = v_ref[r, :]
        q16 = jnp.concatenate(q12, axis=1).astype(BF16)
        k16 = jnp.concatenate(k12, axis=1).astype(BF16)
        scores = lax.dot_general(q16, k16, (((1,), (1,)), ((), ())),
                                 preferred_element_type=F32) * dm_s[...]
        o = jnp.dot(scores.astype(BF16), v, preferred_element_type=F32)
        o = o + jnp.dot(scaled16(q12, t_q), s_ref[...].astype(BF16),
                        preferred_element_type=F32)
        acc_s[r, :] = o
        update_state(s_ref, dec, scaled16(k12, t_k), v)
        qx_s[r, :] = scaled16(q12, t_q_later)
        kx_s[r, :] = scaled16(k12, t_k_later)

    def second_visit(n, s_ref, dec):
        r = pl.ds(pl.multiple_of(n * C, C), C)
        o = acc_s[r, :] + jnp.dot(qx_s[r, :], s_ref[...].astype(BF16),
                                  preferred_element_type=F32)
        y = o * lax.rsqrt(jnp.mean(o * o, axis=-1, keepdims=True) + RMS_EPS)
        o_ref[r, :] = (y * _silu(g_ref[r, :].astype(F32))).astype(o_ref.dtype)
        update_state(s_ref, dec, kx_s[r, :], v_ref[r, :])

    def first_half(i, carry):
        first_visit(i, sf_s, dec_f, _QF, _WF, _QB, _WB)
        first_visit(n_chunks - 1 - i, sb_s, dec_b, _QB, _WB, _QF, _WF)
        return carry

    def second_half(i, carry):
        second_visit(i, sf_s, dec_f)
        second_visit(n_chunks - 1 - i, sb_s, dec_b)
        return carry

    lax.fori_loop(0, n_chunks // 2, first_half, 0, unroll=4)
    lax.fori_loop(n_chunks // 2, n_chunks, second_half, 0, unroll=4)


def _retention(zm, a_f, a_b, cos, sin, batch, seq):
    dh = RET_DH
    assert seq % (2 * RET_CHUNK) == 0
    blk = lambda off: pl.BlockSpec((seq, dh), lambda b, h, o=off // dh: (b, o + h))
    smem = pl.BlockSpec(memory_space=pltpu.SMEM)
    tab = pl.BlockSpec((seq, dh // 2), lambda b, h: (0, 0), pipeline_mode=pl.Buffered(1))
    return pl.pallas_call(
        functools.partial(_ret_kernel, seq=seq),
        grid=(batch, RET_HEADS),
        in_specs=[smem, smem, blk(ZM_RQ), blk(ZM_RK), blk(ZM_RV), blk(ZM_RG), tab, tab],
        out_specs=pl.BlockSpec((seq, dh), lambda b, h: (b, h)),
        out_shape=jax.ShapeDtypeStruct((batch * seq, BRANCH), BF16),
        scratch_shapes=[
            pltpu.VMEM((seq, dh), BF16),
            pltpu.VMEM((seq, dh), BF16),
            pltpu.VMEM((seq, dh), F32),
            pltpu.VMEM((dh, dh), F32),
            pltpu.VMEM((dh, dh), F32),
            pltpu.VMEM((RET_CHUNK, RET_CHUNK), F32),
            pltpu.VMEM((4, RET_CHUNK, dh // 2), F32),
        ],
        compiler_params=_GRID2,
        name="retention",
    )(a_f, a_b, zm, zm, zm, zm, cos, sin)


def _pool_kernel(hw_ref, u_ref, g_ref, cnt_ref, w_ref, sc_ref, o_ref, pad_s, band_s, *, seq, tile):
    M = POOL_MARGIN
    K = tile + 2 * M
    pair = 2 * tile
    hw = hw_ref[pl.program_id(1)]

    pad_s[0:M, :] = jnp.zeros((M, POOL_DG), BF16)
    pad_s[M + seq:M + seq + M, :] = jnp.zeros((M, POOL_DG), BF16)

    def fill(t, carry):
        r = pl.multiple_of(t * pair, pair)
        pad_s[pl.ds(M + r, pair), :] = u_ref[pl.ds(r, pair), :]
        return carry

    lax.fori_loop(0, seq // pair, fill, 0)

    d = (lax.broadcasted_iota(jnp.int32, (tile, K), 1)
         - lax.broadcasted_iota(jnp.int32, (tile, K), 0) - M)
    band_s[...] = jnp.where(d >= -hw, jnp.where(d < hw, 1.0, 0.0), 0.0).astype(BF16)
    w = w_ref[0]
    scale = sc_ref[...]

    def body(t, carry):
        r = pl.multiple_of(t * pair, pair)
        rows = pl.ds(r, pair)
        windows = jnp.concatenate([pad_s[pl.ds(r, K), :], pad_s[pl.ds(r + tile, K), :]], axis=1)
        wsum = jnp.dot(band_s[...], windows, preferred_element_type=F32)
        wsum = jnp.concatenate([wsum[:, :POOL_DG], wsum[:, POOL_DG:]], axis=0)
        cnt = cnt_ref[rows, :]
        cnt = jnp.concatenate([cnt] * (POOL_DG // cnt.shape[1]), axis=1)
        p = wsum / cnt - u_ref[rows, :].astype(F32)
        y = jnp.dot(p.astype(BF16), w, preferred_element_type=F32) * scale
        gate = _silu(g_ref[rows, :].astype(F32))
        o_ref[rows, :] = (y * gate).astype(o_ref.dtype)
        return carry

    lax.fori_loop(0, seq // pair, body, 0, unroll=4)


def _pool(zm, pool_w16, pool_scale, layer, batch, seq, tile=256):
    dg = POOL_DG
    n_groups = len(POOL_WINDOWS)
    assert max(POOL_WINDOWS) // 2 <= POOL_MARGIN and seq >= tile + 2 * POOL_MARGIN
    half_windows = jnp.asarray([w // 2 for w in POOL_WINDOWS], jnp.int32)
    pos = jnp.arange(seq, dtype=jnp.int32)[None, :, None]
    hw3 = half_windows[:, None, None]
    counts = (jnp.minimum(pos + hw3, seq) - jnp.maximum(pos - hw3, 0)).astype(F32)
    counts = jnp.broadcast_to(counts, (n_groups, seq, 128))
    return pl.pallas_call(
        functools.partial(_pool_kernel, seq=seq, tile=tile),
        grid=(batch, n_groups),
        in_specs=[
            pl.BlockSpec(memory_space=pltpu.SMEM),
            pl.BlockSpec((seq, dg), lambda b, g: (b, ZM_PV // dg + g)),
            pl.BlockSpec((seq, dg), lambda b, g: (b, ZM_PG // dg + g)),
            pl.BlockSpec((None, seq, 128), lambda b, g: (g, 0, 0)),
            pl.BlockSpec((None, 1, dg, dg), lambda b, g: (layer, g, 0, 0)),
            pl.BlockSpec((1, dg), lambda b, g: (0, g)),
        ],
        out_specs=pl.BlockSpec((seq, dg), lambda b, g: (b, g)),
        out_shape=jax.ShapeDtypeStruct((batch * seq, BRANCH), BF16),
        scratch_shapes=[pltpu.VMEM((seq + 2 * POOL_MARGIN, dg), BF16),
                        pltpu.VMEM((tile, tile + 2 * POOL_MARGIN), BF16)],
        compiler_params=_GRID2,
        name="pool",
    )(half_windows, zm, zm, counts, pool_w16, pool_scale)


def _att_kernel(sink_ref, q_ref, k_ref, v_ref, g_ref, o_ref, *, seq, nq):
    L = ATT_BLOCK
    G = ATT_GROUP
    kk = pl.program_id(1)
    n = pl.program_id(2)

    row = lax.broadcasted_iota(jnp.int32, (L, 3 * L), 0)
    col = lax.broadcasted_iota(jnp.int32, (L, 3 * L), 1)
    for blk in range(nq):
        nb = n * nq + blk
        rows = slice(blk * L, (blk + 1) * L)
        q = jnp.concatenate([q_ref[rows, h * ATT_DH:(h + 1) * ATT_DH] for h in range(G)],
                            axis=0)
        start = jnp.clip((nb - 1) * L, 0, seq - 3 * L)
        start = pl.multiple_of(start, L)
        kw = k_ref[pl.ds(start, 3 * L), :]
        vw = jnp.concatenate([v_ref[pl.ds(start, 3 * L), :], jnp.ones((3 * L, ATT_DH), BF16)],
                             axis=1)
        s = lax.dot_general(q, kw, (((1,), (1,)), ((), ())),
                            preferred_element_type=F32)
        valid = jnp.abs(col - row + (start - nb * L)) <= ATT_WINDOW
        outs = []
        for h in range(G):
            sh = jnp.where(valid, s[h * L:(h + 1) * L, :], NEG_BIG)
            sk = jnp.full((L, 1), sink_ref[kk * G + h] * LOG2_E, F32)
            m = jnp.maximum(jnp.max(sh, axis=-1, keepdims=True), sk)
            p = jnp.exp2(sh - m)
            pv = jnp.dot(p.astype(BF16), vw, preferred_element_type=F32)
            denom = pv[:, ATT_DH:] + jnp.exp2(sk - m)
            oh = pv[:, :ATT_DH] / denom
            gate = _silu(g_ref[rows, h * ATT_DH:(h + 1) * ATT_DH].astype(F32))
            outs.append((oh * gate).astype(o_ref.dtype))
        o_ref[rows, :] = jnp.concatenate(outs, axis=1)


def _attention(za, zm, sink, batch, seq, nq=8):
    L = ATT_BLOCK
    n_blocks = seq // L
    assert n_blocks % nq == 0
    n_steps = n_blocks // nq
    gw = ATT_GROUP * ATT_DH
    smem = pl.BlockSpec(memory_space=pltpu.SMEM)
    return pl.pallas_call(
        functools.partial(_att_kernel, seq=seq, nq=nq),
        grid=(batch, ATT_KV_HEADS, n_steps),
        in_specs=[
            smem,
            pl.BlockSpec((nq * L, gw), lambda b, k, n: (b * n_steps + n, k)),
            pl.BlockSpec((seq, ATT_DH), lambda b, k, n: (b, ZA_K // ATT_DH + k)),
            pl.BlockSpec((seq, ATT_DH), lambda b, k, n: (b, ZA_V // ATT_DH + k)),
            pl.BlockSpec((nq * L, gw), lambda b, k, n: (b * n_steps + n, ZM_AG // gw + k)),
        ],
        out_specs=pl.BlockSpec((nq * L, gw), lambda b, k, n: (b * n_steps + n, k)),
        out_shape=jax.ShapeDtypeStruct((batch * seq, BRANCH), BF16),
        compiler_params=pltpu.CompilerParams(
            dimension_semantics=("arbitrary", "arbitrary", "arbitrary"),
            vmem_limit_bytes=VMEM_LIMIT),
        name="attention",
    )(sink, za, za, za, zm)


def _merge_kernel(x_ref, ya_ref, yb_ref, yc_ref, *rest, n_col, emit_next):
    gate_refs = rest[:N_GATES * n_col]
    rest = rest[N_GATES * n_col:]
    if emit_next:
        wa_ref, wb_ref, wc_ref, wo_ref, gn_ref, o_ref, hn_ref, m_s = rest
    else:
        wa_ref, wb_ref, wc_ref, wo_ref, o_ref, m_s = rest
    tc = GATE_BLOCK
    ya = ya_ref[...]
    yb = yb_ref[...]
    yc = yc_ref[...]
    for c in range(n_col):
        cs = slice(c * tc, (c + 1) * tc)
        a = jnp.dot(ya, wa_ref[:, cs], preferred_element_type=F32)
        b = jnp.dot(yb, wb_ref[:, cs], preferred_element_type=F32)
        cc = jnp.dot(yc, wc_ref[:, cs], preferred_element_type=F32)
        g0, g1, g2 = (jax.nn.sigmoid(gate_refs[j * n_col + c][...].astype(F32))
                      for j in range(N_GATES))
        m_s[:, cs] = (g0 * a + g1 * b + g2 * cc).astype(BF16)
    x_new = x_ref[...] + jnp.dot(m_s[...], wo_ref[...], preferred_element_type=F32)
    o_ref[...] = x_new
    if emit_next:
        hn_ref[...] = _rms_norm_rows(x_new, gn_ref[...]).astype(hn_ref.dtype)


def _merge(x2, ya, yb, yc, zm, wa, wb, wc, wo, layer, g_next, tm=256):
    m, d = x2.shape
    n_col = d // GATE_BLOCK
    emit_next = g_next is not None
    row = lambda w: pl.BlockSpec((tm, w), lambda i: (i, 0))
    gate = lambda j, c: pl.BlockSpec(
        (tm, GATE_BLOCK), lambda i, o=ZM_MG // GATE_BLOCK + j * n_col + c: (i, o))
    gates = [gate(j, c) for j in range(N_GATES) for c in range(n_col)]
    wspec = lambda k: pl.BlockSpec((None, k, d), lambda i: (layer, 0, 0),
                                   pipeline_mode=pl.Buffered(1))
    in_specs = ([row(d), row(BRANCH), row(BRANCH), row(BRANCH)] + gates
                + [wspec(BRANCH), wspec(BRANCH), wspec(BRANCH), wspec(d)])
    args = [x2, ya, yb, yc] + [zm] * len(gates) + [wa, wb, wc, wo]
    out_specs = row(d)
    out_shape = jax.ShapeDtypeStruct((m, d), F32)
    if emit_next:
        in_specs.append(pl.BlockSpec((1, d), lambda i: (0, 0)))
        args.append(g_next)
        out_specs = (out_specs, row(d))
        out_shape = (out_shape, jax.ShapeDtypeStruct((m, d), BF16))
    return pl.pallas_call(
        functools.partial(_merge_kernel, n_col=n_col, emit_next=emit_next),
        grid=(m // tm,),
        in_specs=in_specs,
        out_specs=out_specs,
        out_shape=out_shape,
        scratch_shapes=[pltpu.VMEM((tm, d), BF16)],
        compiler_params=pltpu.CompilerParams(
            dimension_semantics=("arbitrary",),
            vmem_limit_bytes=VMEM_LIMIT),
        name="merge",
    )(*args)


def _ret_tables(seq):
    half = RET_DH // 2
    inv = 1.0 / (RET_ROPE_BASE ** jnp.linspace(0.0, 1.0, half, dtype=F32))
    ang = jnp.arange(seq, dtype=F32)[:, None] * inv[None, :]
    return jnp.cos(ang), jnp.sin(ang)


def _att_tables(seq):
    half = ROPE_DIMS // 2
    inv = ROPE_THETA ** (-jnp.arange(half, dtype=F32) / half)
    ang = jnp.arange(seq, dtype=F32)[:, None] * inv[None, :]
    c, s = jnp.cos(ang), jnp.sin(ang)
    rest = ATT_DH - ROPE_DIMS
    cos = jnp.concatenate([c, c, jnp.ones((seq, rest), F32)], axis=1)
    sin = jnp.concatenate([-s, s, jnp.zeros((seq, rest), F32)], axis=1)
    return cos, sin


def kernel(x, norm_g, w_in, ret_decay_fwd, ret_decay_bwd, pool_w, pool_scale,
           attn_q_gain, attn_k_gain, attn_sink, w_ret, w_pool, w_att, w_out):
    batch, seq, d = x.shape
    depth = w_in.shape[0]
    assert d == D_MODEL and w_in.shape[2] == IN_WIDTH
    assert seq >= 3 * ATT_BLOCK
    rcos, rsin = _ret_tables(seq)
    acos, asin = _att_tables(seq)
    w_ret16, w_pool16, w_att16, w_out16, pool_w16 = (
        w.astype(BF16) for w in (w_ret, w_pool, w_att, w_out, pool_w))
    x2 = x.reshape(batch * seq, d)
    h = _norm(x2, norm_g[0][None, :])
    for l in range(depth):
        q_gain = attn_q_gain[l] * (ATT_DH ** -0.5 * LOG2_E)
        qk_gains = jnp.concatenate(
            [jnp.tile(q_gain, ATT_Q_HEADS), jnp.tile(attn_k_gain[l], ATT_KV_HEADS),
             jnp.ones((ATT_KV_WIDTH,), F32)])[None, :]
        za = _proj_att(h, w_in, l, qk_gains, acos, asin, seq)
        zm = _proj_main(h, w_in, l)
        ya = _retention(zm, ret_decay_fwd[l], ret_decay_bwd[l], rcos, rsin, batch, seq)
        yb = _pool(zm, pool_w16, pool_scale[l][None, :], l, batch, seq)
        yc = _attention(za, zm, attn_sink[l], batch, seq)
        g_next = norm_g[l + 1][None, :] if l + 1 < depth else None
        res = _merge(x2, ya, yb, yc, zm, w_ret16, w_pool16, w_att16, w_out16, l, g_next)
        x2, h = res if g_next is not None else (res, None)
    return x2.reshape(batch, seq, d)
```

```python
import functools

import jax
import jax.numpy as jnp
from jax import lax
from jax.experimental import pallas as pl
from jax.experimental.pallas import tpu as pltpu

F32 = jnp.float32
BF16 = jnp.bfloat16

D_MODEL = 2048
BRANCH = D_MODEL // 2
RET_HEADS = 4
RET_DH = BRANCH // RET_HEADS
RET_CHUNK = 256
RET_ROPE_BASE = 10000.0
POOL_WINDOWS = (2, 4, 8, 16)
POOL_DG = BRANCH // len(POOL_WINDOWS)
POOL_MARGIN = 128
ATT_DH = 128
ATT_Q_HEADS = BRANCH // ATT_DH
ATT_KV_HEADS = ATT_Q_HEADS // 4
ATT_GROUP = ATT_Q_HEADS // ATT_KV_HEADS
ATT_KV_WIDTH = ATT_KV_HEADS * ATT_DH
ATT_BLOCK = 128
ATT_WINDOW = 128
ROPE_THETA = 500000.0
ROPE_DIMS = ATT_DH // 4
RMS_EPS = 1e-6
NEG_BIG = -1e30
LOG2_E = 1.4426950408889634
N_GATES = 3
MG_WIDTH = N_GATES * D_MODEL

W_AQ = 6 * BRANCH
W_AG = W_AQ + BRANCH + 2 * ATT_KV_WIDTH
IN_WIDTH = W_AG + BRANCH + MG_WIDTH

ZM_RQ, ZM_RK, ZM_RV, ZM_RG, ZM_PV, ZM_PG, ZM_AG, ZM_MG = (k * BRANCH for k in range(8))
ZM_WIDTH = ZM_MG + MG_WIDTH
ZA_WIDTH = BRANCH + 2 * ATT_KV_WIDTH
ZA_K = BRANCH
ZA_V = BRANCH + ATT_KV_WIDTH
GATE_BLOCK = 1024

VMEM_LIMIT = 56 * 1024 * 1024


def _silu(x):
    return x * jax.nn.sigmoid(x)


def _rms_norm_rows(x, gain):
    ms = jnp.mean(x * x, axis=-1, keepdims=True)
    return x * lax.rsqrt(ms + RMS_EPS) * gain


def _norm_kernel(x_ref, g_ref, h_ref):
    h_ref[...] = _rms_norm_rows(x_ref[...], g_ref[...]).astype(h_ref.dtype)


def _norm(x2, g, tm=512):
    m, d = x2.shape
    return pl.pallas_call(
        _norm_kernel,
        grid=(m // tm,),
        in_specs=[pl.BlockSpec((tm, d), lambda i: (i, 0)),
                  pl.BlockSpec((1, d), lambda i: (0, 0))],
        out_specs=pl.BlockSpec((tm, d), lambda i: (i, 0)),
        out_shape=jax.ShapeDtypeStruct((m, d), BF16),
        compiler_params=pltpu.CompilerParams(
            dimension_semantics=("arbitrary",), vmem_limit_bytes=VMEM_LIMIT),
        name="norm",
    )(x2, g)


_GRID2 = pltpu.CompilerParams(dimension_semantics=("arbitrary", "arbitrary"),
                              vmem_limit_bytes=VMEM_LIMIT)


def _proj_main_kernel(h_ref, *refs):
    w_refs, z_ref, w16_s = refs[:-2], refs[-2], refs[-1]
    tn = w_refs[0].shape[1]

    @pl.when(pl.program_id(1) == 0)
    def _():
        for k, w_ref in enumerate(w_refs):
            w16_s[k] = w_ref[...].astype(BF16)

    h = h_ref[...]
    for k in range(len(w_refs)):
        z_ref[:, k * tn:(k + 1) * tn] = jnp.dot(
            h, w16_s[k], preferred_element_type=F32).astype(z_ref.dtype)


def _proj_main(h, w_all, layer, tm=2048, tn=512, tiles_per_step=2):
    m, d = h.shape
    n_before = W_AQ // tn
    skip = (W_AG - W_AQ) // tn
    n_tiles = ZM_WIDTH // tn
    assert n_tiles % tiles_per_step == 0

    def w_spec(k):
        def index(j, i):
            tile = j * tiles_per_step + k
            return layer, 0, jnp.where(tile < n_before, tile, tile + skip)
        return pl.BlockSpec((None, d, tn), index)

    return pl.pallas_call(
        _proj_main_kernel,
        grid=(n_tiles // tiles_per_step, m // tm),
        in_specs=[pl.BlockSpec((tm, d), lambda j, i: (i, 0))]
                 + [w_spec(k) for k in range(tiles_per_step)],
        out_specs=pl.BlockSpec((tm, tiles_per_step * tn), lambda j, i: (i, j)),
        out_shape=jax.ShapeDtypeStruct((m, ZM_WIDTH), BF16),
        scratch_shapes=[pltpu.VMEM((tiles_per_step, d, tn), BF16)],
        compiler_params=_GRID2,
        name="proj_main",
    )(h, *([w_all] * tiles_per_step))


def _rope_partial(x, cos, sin_signed):
    lane = lax.broadcasted_iota(jnp.int32, x.shape, 1)
    up = pltpu.roll(x, ATT_DH - ROPE_DIMS // 2, 1)
    down = pltpu.roll(x, ROPE_DIMS // 2, 1)
    partner = jnp.where(lane < ROPE_DIMS // 2, up, down)
    return x * cos + partner * sin_signed


def _proj_att_kernel(h_ref, w_ref, gain_ref, cos_ref, sin_ref, z_ref, acc_s, w16_s, *,
                     n_row, heads_per_tile, n_qk_heads):
    t = pl.program_id(0)

    @pl.when(t == 0)
    def _():
        acc_s[...] = jnp.zeros_like(acc_s)

    @pl.when(t % n_row == 0)
    def _():
        w16_s[...] = w_ref[...].astype(BF16)

    first_head = (jnp.maximum(t - 1, 0) // n_row) * heads_per_tile
    prev = acc_s[...]
    acc_s[...] = jnp.dot(h_ref[...], w16_s[...], preferred_element_type=F32)
    cos = cos_ref[...]
    sin = sin_ref[...]
    outs = []
    for hh in range(heads_per_tile):
        cols = slice(hh * ATT_DH, (hh + 1) * ATT_DH)
        xh = prev[:, cols]
        qk = _rope_partial(_rms_norm_rows(xh, gain_ref[:, cols]), cos, sin)
        outs.append(jnp.where(first_head + hh < n_qk_heads, qk, xh))
    z_ref[...] = jnp.concatenate(outs, axis=1).astype(z_ref.dtype)


def _proj_att(h, w_all, layer, gains, cos, sin, seq, tm=2048, tn=512):
    m, d = h.shape
    n_row = m // tm
    n_tiles = n_row * (ZA_WIDTH // tn)
    pos_tiles = seq // tm

    def cur(t):
        tt = jnp.minimum(t, n_tiles - 1)
        return tt % n_row, tt // n_row

    def prev(t):
        tt = jnp.maximum(t - 1, 0)
        return tt % n_row, tt // n_row

    tab = pl.BlockSpec((tm, ATT_DH), lambda t: (prev(t)[0] % pos_tiles, 0))
    return pl.pallas_call(
        functools.partial(_proj_att_kernel, n_row=n_row, heads_per_tile=tn // ATT_DH,
                          n_qk_heads=ZA_V // ATT_DH),
        grid=(n_tiles + 1,),
        in_specs=[
            pl.BlockSpec((tm, d), lambda t: (cur(t)[0], 0)),
            pl.BlockSpec((None, d, tn), lambda t: (layer, 0, W_AQ // tn + cur(t)[1])),
            pl.BlockSpec((1, tn), lambda t: (0, prev(t)[1])),
            tab, tab,
        ],
        out_specs=pl.BlockSpec((tm, tn), lambda t: prev(t)),
        out_shape=jax.ShapeDtypeStruct((m, ZA_WIDTH), BF16),
        scratch_shapes=[pltpu.VMEM((tm, tn), F32), pltpu.VMEM((d, tn), BF16)],
        compiler_params=pltpu.CompilerParams(dimension_semantics=("arbitrary",),
                                             vmem_limit_bytes=VMEM_LIMIT),
        name="proj_att",
    )(h, w_all, gains, cos, sin)


_QF, _QB, _WF, _WB = range(4)


def _ret_kernel(af_ref, ab_ref, q_ref, k_ref, v_ref, g_ref, cos_ref, sin_ref,
                o_ref, qx_s, kx_s, acc_s, sf_s, sb_s, dm_s, tab_s, *, seq):
    C = RET_CHUNK
    n_chunks = seq // C
    half = RET_DH // 2
    h = pl.program_id(1)

    lg_f = -jnp.exp(jnp.full((C, half), af_ref[h], F32))
    lg_b = -jnp.exp(jnp.full((C, half), ab_ref[h], F32))
    pos = lax.broadcasted_iota(jnp.int32, (C, half), 0).astype(F32)
    tab_s[_QF] = jnp.exp(lg_f * (pos + 1.0))
    tab_s[_QB] = jnp.exp(lg_b * (C - pos))
    k_scale = RET_DH ** -0.5
    tab_s[_WF] = jnp.exp(lg_f * (C - 1.0 - pos)) * k_scale
    tab_s[_WB] = jnp.exp(lg_b * pos) * k_scale
    dec_f = jnp.exp(lg_f[:1] * C)
    dec_b = jnp.exp(lg_b[:1] * C)
    dec_f = jnp.concatenate([dec_f, dec_f], axis=1)
    dec_b = jnp.concatenate([dec_b, dec_b], axis=1)
    for c in range(C // half):
        row = lax.broadcasted_iota(jnp.int32, (C, half), 0)
        col = lax.broadcasted_iota(jnp.int32, (C, half), 1) + c * half
        lag = (row - col).astype(F32)
        alag = jnp.abs(lag)
        dm_s[:, c * half:(c + 1) * half] = jnp.where(
            lag >= 0, jnp.exp(lg_f * alag), jnp.exp(lg_b * alag)) * k_scale
    sf_s[...] = jnp.zeros_like(sf_s)
    sb_s[...] = jnp.zeros_like(sb_s)

    def rot(x, cos, sin):
        x1, x2 = x[:, :half], x[:, half:]
        return x1 * cos - x2 * sin, x2 * cos + x1 * sin

    def scaled16(x12, t):
        tab = tab_s[t]
        return jnp.concatenate([x12[0] * tab, x12[1] * tab], axis=1).astype(BF16)

    def update_state(s_ref, dec, k16, v):
        kv = lax.dot_general(k16, v, (((0,), (0,)), ((), ())), preferred_element_type=F32)
        s_ref[...] = s_ref[...] * dec + kv

    def first_visit(n, s_ref, dec, t_q, t_k, t_q_later, t_k_later):
        r = pl.ds(pl.multiple_of(n * C, C), C)
        cos = cos_ref[r, :]
        sin = sin_ref[r, :]
        q12 = rot(q_ref[r, :].astype(F32), cos, sin)
        k12 = rot(k_ref[r, :].astype(F32), cos, sin)
        v = v_ref[r, :]
        q16 = jnp.concatenate(q12, axis=1).astype(BF16)
        k16 = jnp.concatenate(k12, axis=1).astype(BF16)
        scores = lax.dot_general(q16, k16, (((1,), (1,)), ((), ())),
                                 preferred_element_type=F32) * dm_s[...]
        o = jnp.dot(scores.astype(BF16), v, preferred_element_type=F32)
        o = o + jnp.dot(scaled16(q12, t_q), s_ref[...].astype(BF16),
                        preferred_element_type=F32)
        acc_s[r, :] = o
        update_state(s_ref, dec, scaled16(k12, t_k), v)
        qx_s[r, :] = scaled16(q12, t_q_later)
        kx_s[r, :] = scaled16(k12, t_k_later)

    def second_visit(n, s_ref, dec):
        r = pl.ds(pl.multiple_of(n * C, C), C)
        o = acc_s[r, :] + jnp.dot(qx_s[r, :], s_ref[...].astype(BF16),
                                  preferred_element_type=F32)
        y = o * lax.rsqrt(jnp.mean(o * o, axis=-1, keepdims=True) + RMS_EPS)
        o_ref[r, :] = (y * _silu(g_ref[r, :].astype(F32))).astype(o_ref.dtype)
        update_state(s_ref, dec, kx_s[r, :], v_ref[r, :])

    def first_half(i, carry):
        first_visit(i, sf_s, dec_f, _QF, _WF, _QB, _WB)
        first_visit(n_chunks - 1 - i, sb_s, dec_b, _QB, _WB, _QF, _WF)
        return carry

    def second_half(i, carry):
        second_visit(i, sf_s, dec_f)
        second_visit(n_chunks - 1 - i, sb_s, dec_b)
        return carry

    lax.fori_loop(0, n_chunks // 2, first_half, 0, unroll=True)
    lax.fori_loop(n_chunks // 2, n_chunks, second_half, 0, unroll=True)


def _retention(zm, a_f, a_b, cos, sin, batch, seq):
    dh = RET_DH
    assert seq % (2 * RET_CHUNK) == 0
    blk = lambda off: pl.BlockSpec((seq, dh), lambda b, h, o=off // dh: (b, o + h))
    smem = pl.BlockSpec(memory_space=pltpu.SMEM)
    tab = pl.BlockSpec((seq, dh // 2), lambda b, h: (0, 0), pipeline_mode=pl.Buffered(1))
    return pl.pallas_call(
        functools.partial(_ret_kernel, seq=seq),
        grid=(batch, RET_HEADS),
        in_specs=[smem, smem, blk(ZM_RQ), blk(ZM_RK), blk(ZM_RV), blk(ZM_RG), tab, tab],
        out_specs=pl.BlockSpec((seq, dh), lambda b, h: (b, h)),
        out_shape=jax.ShapeDtypeStruct((batch * seq, BRANCH), BF16),
        scratch_shapes=[
            pltpu.VMEM((seq, dh), BF16),
            pltpu.VMEM((seq, dh), BF16),
            pltpu.VMEM((seq, dh), F32),
            pltpu.VMEM((dh, dh), F32),
            pltpu.VMEM((dh, dh), F32),
            pltpu.VMEM((RET_CHUNK, RET_CHUNK), F32),
            pltpu.VMEM((4, RET_CHUNK, dh // 2), F32),
        ],
        compiler_params=_GRID2,
        name="retention",
    )(a_f, a_b, zm, zm, zm, zm, cos, sin)


def _pool_kernel(hw_ref, u_ref, g_ref, cnt_ref, w_ref, sc_ref, o_ref, pad_s, band_s, *, seq, tile):
    M = POOL_MARGIN
    K = tile + 2 * M
    pair = 2 * tile
    hw = hw_ref[pl.program_id(1)]

    pad_s[0:M, :] = jnp.zeros((M, POOL_DG), BF16)
    pad_s[M + seq:M + seq + M, :] = jnp.zeros((M, POOL_DG), BF16)

    def fill(t, carry):
        r = pl.multiple_of(t * pair, pair)
        pad_s[pl.ds(M + r, pair), :] = u_ref[pl.ds(r, pair), :]
        return carry

    lax.fori_loop(0, seq // pair, fill, 0)

    d = (lax.broadcasted_iota(jnp.int32, (tile, K), 1)
         - lax.broadcasted_iota(jnp.int32, (tile, K), 0) - M)
    band_s[...] = jnp.where(d >= -hw, jnp.where(d < hw, 1.0, 0.0), 0.0).astype(BF16)
    w = w_ref[0]
    scale = sc_ref[...]

    def body(t, carry):
        r = pl.multiple_of(t * pair, pair)
        rows = pl.ds(r, pair)
        windows = jnp.concatenate([pad_s[pl.ds(r, K), :], pad_s[pl.ds(r + tile, K), :]], axis=1)
        wsum = jnp.dot(band_s[...], windows, preferred_element_type=F32)
        wsum = jnp.concatenate([wsum[:, :POOL_DG], wsum[:, POOL_DG:]], axis=0)
        cnt = cnt_ref[rows, :]
        cnt = jnp.concatenate([cnt] * (POOL_DG // cnt.shape[1]), axis=1)
        p = wsum / cnt - u_ref[rows, :].astype(F32)
        y = jnp.dot(p.astype(BF16), w, preferred_element_type=F32) * scale
        gate = _silu(g_ref[rows, :].astype(F32))
        o_ref[rows, :] = (y * gate).astype(o_ref.dtype)
        return carry

    lax.fori_loop(0, seq // pair, body, 0, unroll=4)


def _pool(zm, pool_w16, pool_scale, layer, batch, seq, tile=256):
    dg = POOL_DG
    n_groups = len(POOL_WINDOWS)
    assert max(POOL_WINDOWS) // 2 <= POOL_MARGIN and seq >= tile + 2 * POOL_MARGIN
    half_windows = jnp.asarray([w // 2 for w in POOL_WINDOWS], jnp.int32)
    pos = jnp.arange(seq, dtype=jnp.int32)[None, :, None]
    hw3 = half_windows[:, None, None]
    counts = (jnp.minimum(pos + hw3, seq) - jnp.maximum(pos - hw3, 0)).astype(F32)
    counts = jnp.broadcast_to(counts, (n_groups, seq, 128))
    return pl.pallas_call(
        functools.partial(_pool_kernel, seq=seq, tile=tile),
        grid=(batch, n_groups),
        in_specs=[
            pl.BlockSpec(memory_space=pltpu.SMEM),
            pl.BlockSpec((seq, dg), lambda b, g: (b, ZM_PV // dg + g)),
            pl.BlockSpec((seq, dg), lambda b, g: (b, ZM_PG // dg + g)),
            pl.BlockSpec((None, seq, 128), lambda b, g: (g, 0, 0)),
            pl.BlockSpec((None, 1, dg, dg), lambda b, g: (layer, g, 0, 0)),
            pl.BlockSpec((1, dg), lambda b, g: (0, g)),
        ],
        out_specs=pl.BlockSpec((seq, dg), lambda b, g: (b, g)),
        out_shape=jax.ShapeDtypeStruct((batch * seq, BRANCH), BF16),
        scratch_shapes=[pltpu.VMEM((seq + 2 * POOL_MARGIN, dg), BF16),
                        pltpu.VMEM((tile, tile + 2 * POOL_MARGIN), BF16)],
        compiler_params=_GRID2,
        name="pool",
    )(half_windows, zm, zm, counts, pool_w16, pool_scale)


def _att_kernel(sink_ref, q_ref, k_ref, v_ref, g_ref, o_ref, *, seq, nq):
    L = ATT_BLOCK
    G = ATT_GROUP
    kk = pl.program_id(1)
    n = pl.program_id(2)

    row = lax.broadcasted_iota(jnp.int32, (L, 3 * L), 0)
    col = lax.broadcasted_iota(jnp.int32, (L, 3 * L), 1)
    for blk in range(nq):
        nb = n * nq + blk
        rows = slice(blk * L, (blk + 1) * L)
        q = jnp.concatenate([q_ref[rows, h * ATT_DH:(h + 1) * ATT_DH] for h in range(G)],
                            axis=0)
        start = jnp.clip((nb - 1) * L, 0, seq - 3 * L)
        start = pl.multiple_of(start, L)
        kw = k_ref[pl.ds(start, 3 * L), :]
        vw = jnp.concatenate([v_ref[pl.ds(start, 3 * L), :], jnp.ones((3 * L, ATT_DH), BF16)],
                             axis=1)
        s = lax.dot_general(q, kw, (((1,), (1,)), ((), ())),
                            preferred_element_type=F32)
        valid = jnp.abs(col - row + (start - nb * L)) <= ATT_WINDOW
        outs = []
        for h in range(G):
            sh = jnp.where(valid, s[h * L:(h + 1) * L, :], NEG_BIG)
            sk = jnp.full((L, 1), sink_ref[kk * G + h] * LOG2_E, F32)
            m = jnp.maximum(jnp.max(sh, axis=-1, keepdims=True), sk)
            p = jnp.exp2(sh - m)
            pv = jnp.dot(p.astype(BF16), vw, preferred_element_type=F32)
            denom = pv[:, ATT_DH:] + jnp.exp2(sk - m)
            oh = pv[:, :ATT_DH] / denom
            gate = _silu(g_ref[rows, h * ATT_DH:(h + 1) * ATT_DH].astype(F32))
            outs.append((oh * gate).astype(o_ref.dtype))
        o_ref[rows, :] = jnp.concatenate(outs, axis=1)


def _attention(za, zm, sink, batch, seq, nq=16):
    L = ATT_BLOCK
    n_blocks = seq // L
    assert n_blocks % nq == 0
    n_steps = n_blocks // nq
    gw = ATT_GROUP * ATT_DH
    smem = pl.BlockSpec(memory_space=pltpu.SMEM)
    return pl.pallas_call(
        functools.partial(_att_kernel, seq=seq, nq=nq),
        grid=(batch, ATT_KV_HEADS, n_steps),
        in_specs=[
            smem,
            pl.BlockSpec((nq * L, gw), lambda b, k, n: (b * n_steps + n, k)),
            pl.BlockSpec((seq, ATT_DH), lambda b, k, n: (b, ZA_K // ATT_DH + k)),
            pl.BlockSpec((seq, ATT_DH), lambda b, k, n: (b, ZA_V // ATT_DH + k)),
            pl.BlockSpec((nq * L, gw), lambda b, k, n: (b * n_steps + n, ZM_AG // gw + k)),
        ],
        out_specs=pl.BlockSpec((nq * L, gw), lambda b, k, n: (b * n_steps + n, k)),
        out_shape=jax.ShapeDtypeStruct((batch * seq, BRANCH), BF16),
        compiler_params=pltpu.CompilerParams(
            dimension_semantics=("arbitrary", "arbitrary", "arbitrary"),
            vmem_limit_bytes=VMEM_LIMIT),
        name="attention",
    )(sink, za, za, za, zm)


def _merge_kernel(x_ref, ya_ref, yb_ref, yc_ref, *rest, n_col, emit_next):
    gate_refs = rest[:N_GATES * n_col]
    rest = rest[N_GATES * n_col:]
    if emit_next:
        wa_ref, wb_ref, wc_ref, wo_ref, gn_ref, o_ref, hn_ref, m_s = rest
    else:
        wa_ref, wb_ref, wc_ref, wo_ref, o_ref, m_s = rest
    tc = GATE_BLOCK
    ya = ya_ref[...]
    yb = yb_ref[...]
    yc = yc_ref[...]
    for c in range(n_col):
        cs = slice(c * tc, (c + 1) * tc)
        a = jnp.dot(ya, wa_ref[:, cs], preferred_element_type=F32)
        b = jnp.dot(yb, wb_ref[:, cs], preferred_element_type=F32)
        cc = jnp.dot(yc, wc_ref[:, cs], preferred_element_type=F32)
        g0, g1, g2 = (jax.nn.sigmoid(gate_refs[j * n_col + c][...].astype(F32))
                      for j in range(N_GATES))
        m_s[:, cs] = (g0 * a + g1 * b + g2 * cc).astype(BF16)
    x_new = x_ref[...] + jnp.dot(m_s[...], wo_ref[...], preferred_element_type=F32)
    o_ref[...] = x_new
    if emit_next:
        hn_ref[...] = _rms_norm_rows(x_new, gn_ref[...]).astype(hn_ref.dtype)


def _merge(x2, ya, yb, yc, zm, wa, wb, wc, wo, layer, g_next, tm=256):
    m, d = x2.shape
    n_col = d // GATE_BLOCK
    emit_next = g_next is not None
    row = lambda w: pl.BlockSpec((tm, w), lambda i: (i, 0))
    gate = lambda j, c: pl.BlockSpec(
        (tm, GATE_BLOCK), lambda i, o=ZM_MG // GATE_BLOCK + j * n_col + c: (i, o))
    gates = [gate(j, c) for j in range(N_GATES) for c in range(n_col)]
    wspec = lambda k: pl.BlockSpec((None, k, d), lambda i: (layer, 0, 0),
                                   pipeline_mode=pl.Buffered(1))
    in_specs = ([row(d), row(BRANCH), row(BRANCH), row(BRANCH)] + gates
                + [wspec(BRANCH), wspec(BRANCH), wspec(BRANCH), wspec(d)])
    args = [x2, ya, yb, yc] + [zm] * len(gates) + [wa, wb, wc, wo]
    out_specs = row(d)
    out_shape = jax.ShapeDtypeStruct((m, d), F32)
    if emit_next:
        in_specs.append(pl.BlockSpec((1, d), lambda i: (0, 0)))
        args.append(g_next)
        out_specs = (out_specs, row(d))
        out_shape = (out_shape, jax.ShapeDtypeStruct((m, d), BF16))
    return pl.pallas_call(
        functools.partial(_merge_kernel, n_col=n_col, emit_next=emit_next),
        grid=(m // tm,),
        in_specs=in_specs,
        out_specs=out_specs,
        out_shape=out_shape,
        scratch_shapes=[pltpu.VMEM((tm, d), BF16)],
        compiler_params=pltpu.CompilerParams(
            dimension_semantics=("arbitrary",),
            vmem_limit_bytes=VMEM_LIMIT),
        name="merge",
    )(*args)


def _ret_tables(seq):
    half = RET_DH // 2
    inv = 1.0 / (RET_ROPE_BASE ** jnp.linspace(0.0, 1.0, half, dtype=F32))
    ang = jnp.arange(seq, dtype=F32)[:, None] * inv[None, :]
    return jnp.cos(ang), jnp.sin(ang)


def _att_tables(seq):
    half = ROPE_DIMS // 2
    inv = ROPE_THETA ** (-jnp.arange(half, dtype=F32) / half)
    ang = jnp.arange(seq, dtype=F32)[:, None] * inv[None, :]
    c, s = jnp.cos(ang), jnp.sin(ang)
    rest = ATT_DH - ROPE_DIMS
    cos = jnp.concatenate([c, c, jnp.ones((seq, rest), F32)], axis=1)
    sin = jnp.concatenate([-s, s, jnp.zeros((seq, rest), F32)], axis=1)
    return cos, sin


def kernel(x, norm_g, w_in, ret_decay_fwd, ret_decay_bwd, pool_w, pool_scale,
           attn_q_gain, attn_k_gain, attn_sink, w_ret, w_pool, w_att, w_out):
    batch, seq, d = x.shape
    depth = w_in.shape[0]
    assert d == D_MODEL and w_in.shape[2] == IN_WIDTH
    assert seq >= 3 * ATT_BLOCK
    rcos, rsin = _ret_tables(seq)
    acos, asin = _att_tables(seq)
    w_ret16, w_pool16, w_att16, w_out16, pool_w16 = (
        w.astype(BF16) for w in (w_ret, w_pool, w_att, w_out, pool_w))
    x2 = x.reshape(batch * seq, d)
    h = _norm(x2, norm_g[0][None, :])
    for l in range(depth):
        q_gain = attn_q_gain[l] * (ATT_DH ** -0.5 * LOG2_E)
        qk_gains = jnp.concatenate(
            [jnp.tile(q_gain, ATT_Q_HEADS), jnp.tile(attn_k_gain[l], ATT_KV_HEADS),
             jnp.ones((ATT_KV_WIDTH,), F32)])[None, :]
        za = _proj_att(h, w_in, l, qk_gains, acos, asin, seq)
        zm = _proj_main(h, w_in, l)
        ya = _retention(zm, ret_decay_fwd[l], ret_decay_bwd[l], rcos, rsin, batch, seq)
        yb = _pool(zm, pool_w16, pool_scale[l][None, :], l, batch, seq)
        yc = _attention(za, zm, attn_sink[l], batch, seq)
        g_next = norm_g[l + 1][None, :] if l + 1 < depth else None
        res = _merge(x2, ya, yb, yc, zm, w_ret16, w_pool16, w_att16, w_out16, l, g_next)
        x2, h = res if g_next is not None else (res, None)
    return x2.reshape(batch, seq, d)
```

```python
import functools

import jax
import jax.numpy as jnp
from jax import lax
from jax.experimental import pallas as pl
from jax.experimental.pallas import tpu as pltpu

F32 = jnp.float32
BF16 = jnp.bfloat16

D_MODEL = 2048
BRANCH = D_MODEL // 2
RET_HEADS = 4
RET_DH = BRANCH // RET_HEADS
RET_CHUNK = 256
RET_ROPE_BASE = 10000.0
POOL_WINDOWS = (2, 4, 8, 16)
POOL_DG = BRANCH // len(POOL_WINDOWS)
POOL_MARGIN = 128
ATT_DH = 128
ATT_Q_HEADS = BRANCH // ATT_DH
ATT_KV_HEADS = ATT_Q_HEADS // 4
ATT_GROUP = ATT_Q_HEADS // ATT_KV_HEADS
ATT_KV_WIDTH = ATT_KV_HEADS * ATT_DH
ATT_BLOCK = 128
ATT_WINDOW = 128
ROPE_THETA = 500000.0
ROPE_DIMS = ATT_DH // 4
RMS_EPS = 1e-6
NEG_BIG = -1e30
LOG2_E = 1.4426950408889634
N_GATES = 3
MG_WIDTH = N_GATES * D_MODEL

W_AQ = 6 * BRANCH
W_AG = W_AQ + BRANCH + 2 * ATT_KV_WIDTH
IN_WIDTH = W_AG + BRANCH + MG_WIDTH

ZM_RQ, ZM_RK, ZM_RV, ZM_RG, ZM_PV, ZM_PG, ZM_AG, ZM_MG = (k * BRANCH for k in range(8))
ZM_WIDTH = ZM_MG + MG_WIDTH
ZA_WIDTH = BRANCH + 2 * ATT_KV_WIDTH
ZA_K = BRANCH
ZA_V = BRANCH + ATT_KV_WIDTH
GATE_BLOCK = 1024

VMEM_LIMIT = 56 * 1024 * 1024


def _silu(x):
    return x * jax.nn.sigmoid(x)


def _rms_norm_rows(x, gain):
    ms = jnp.mean(x * x, axis=-1, keepdims=True)
    return x * lax.rsqrt(ms + RMS_EPS) * gain


def _norm_kernel(x_ref, g_ref, h_ref):
    h_ref[...] = _rms_norm_rows(x_ref[...], g_ref[...]).astype(h_ref.dtype)


def _norm(x2, g, tm=512):
    m, d = x2.shape
    return pl.pallas_call(
        _norm_kernel,
        grid=(m // tm,),
        in_specs=[pl.BlockSpec((tm, d), lambda i: (i, 0)),
                  pl.BlockSpec((1, d), lambda i: (0, 0))],
        out_specs=pl.BlockSpec((tm, d), lambda i: (i, 0)),
        out_shape=jax.ShapeDtypeStruct((m, d), BF16),
        compiler_params=pltpu.CompilerParams(
            dimension_semantics=("arbitrary",), vmem_limit_bytes=VMEM_LIMIT),
        name="norm",
    )(x2, g)


_GRID2 = pltpu.CompilerParams(dimension_semantics=("arbitrary", "arbitrary"),
                              vmem_limit_bytes=VMEM_LIMIT)


def _proj_main_kernel(h_ref, *refs):
    w_refs, z_ref, w16_s = refs[:-2], refs[-2], refs[-1]
    tn = w_refs[0].shape[1]

    @pl.when(pl.program_id(1) == 0)
    def _():
        for k, w_ref in enumerate(w_refs):
            w16_s[k] = w_ref[...].astype(BF16)

    h = h_ref[...]
    for k in range(len(w_refs)):
        z_ref[:, k * tn:(k + 1) * tn] = jnp.dot(
            h, w16_s[k], preferred_element_type=F32).astype(z_ref.dtype)


def _proj_main(h, w_all, layer, tm=2048, tn=512, tiles_per_step=2):
    m, d = h.shape
    n_before = W_AQ // tn
    skip = (W_AG - W_AQ) // tn
    n_tiles = ZM_WIDTH // tn
    assert n_tiles % tiles_per_step == 0

    def w_spec(k):
        def index(j, i):
            tile = j * tiles_per_step + k
            return layer, 0, jnp.where(tile < n_before, tile, tile + skip)
        return pl.BlockSpec((None, d, tn), index)

    return pl.pallas_call(
        _proj_main_kernel,
        grid=(n_tiles // tiles_per_step, m // tm),
        in_specs=[pl.BlockSpec((tm, d), lambda j, i: (i, 0))]
                 + [w_spec(k) for k in range(tiles_per_step)],
        out_specs=pl.BlockSpec((tm, tiles_per_step * tn), lambda j, i: (i, j)),
        out_shape=jax.ShapeDtypeStruct((m, ZM_WIDTH), BF16),
        scratch_shapes=[pltpu.VMEM((tiles_per_step, d, tn), BF16)],
        compiler_params=_GRID2,
        name="proj_main",
    )(h, *([w_all] * tiles_per_step))


def _rope_partial(x, cos, sin_signed):
    lane = lax.broadcasted_iota(jnp.int32, x.shape, 1)
    up = pltpu.roll(x, ATT_DH - ROPE_DIMS // 2, 1)
    down = pltpu.roll(x, ROPE_DIMS // 2, 1)
    partner = jnp.where(lane < ROPE_DIMS // 2, up, down)
    return x * cos + partner * sin_signed


def _proj_att_kernel(h_ref, w_ref, gain_ref, cos_ref, sin_ref, z_ref, acc_s, w16_s, *,
                     n_row, heads_per_tile, n_qk_heads):
    t = pl.program_id(0)

    @pl.when(t == 0)
    def _():
        acc_s[...] = jnp.zeros_like(acc_s)

    @pl.when(t % n_row == 0)
    def _():
        w16_s[...] = w_ref[...].astype(BF16)

    first_head = (jnp.maximum(t - 1, 0) // n_row) * heads_per_tile
    prev = acc_s[...]
    acc_s[...] = jnp.dot(h_ref[...], w16_s[...], preferred_element_type=F32)
    cos = cos_ref[...]
    sin = sin_ref[...]
    outs = []
    for hh in range(heads_per_tile):
        cols = slice(hh * ATT_DH, (hh + 1) * ATT_DH)
        xh = prev[:, cols]
        qk = _rope_partial(_rms_norm_rows(xh, gain_ref[:, cols]), cos, sin)
        outs.append(jnp.where(first_head + hh < n_qk_heads, qk, xh))
    z_ref[...] = jnp.concatenate(outs, axis=1).astype(z_ref.dtype)


def _proj_att(h, w_all, layer, gains, cos, sin, seq, tm=2048, tn=512):
    m, d = h.shape
    n_row = m // tm
    n_tiles = n_row * (ZA_WIDTH // tn)
    pos_tiles = seq // tm

    def cur(t):
        tt = jnp.minimum(t, n_tiles - 1)
        return tt % n_row, tt // n_row

    def prev(t):
        tt = jnp.maximum(t - 1, 0)
        return tt % n_row, tt // n_row

    tab = pl.BlockSpec((tm, ATT_DH), lambda t: (prev(t)[0] % pos_tiles, 0))
    return pl.pallas_call(
        functools.partial(_proj_att_kernel, n_row=n_row, heads_per_tile=tn // ATT_DH,
                          n_qk_heads=ZA_V // ATT_DH),
        grid=(n_tiles + 1,),
        in_specs=[
            pl.BlockSpec((tm, d), lambda t: (cur(t)[0], 0)),
            pl.BlockSpec((None, d, tn), lambda t: (layer, 0, W_AQ // tn + cur(t)[1])),
            pl.BlockSpec((1, tn), lambda t: (0, prev(t)[1])),
            tab, tab,
        ],
        out_specs=pl.BlockSpec((tm, tn), lambda t: prev(t)),
        out_shape=jax.ShapeDtypeStruct((m, ZA_WIDTH), BF16),
        scratch_shapes=[pltpu.VMEM((tm, tn), F32), pltpu.VMEM((d, tn), BF16)],
        compiler_params=pltpu.CompilerParams(dimension_semantics=("arbitrary",),
                                             vmem_limit_bytes=VMEM_LIMIT),
        name="proj_att",
    )(h, w_all, gains, cos, sin)


_QF, _QB, _WF, _WB = range(4)


def _ret_kernel(af_ref, ab_ref, q_ref, k_ref, v_ref, g_ref, cos_ref, sin_ref,
                o_ref, qx_s, kx_s, acc_s, sf_s, sb_s, dm_s, tab_s, *, seq):
    C = RET_CHUNK
    n_chunks = seq // C
    half = RET_DH // 2
    h = pl.program_id(1)

    lg_f = -jnp.exp(jnp.full((C, half), af_ref[h], F32))
    lg_b = -jnp.exp(jnp.full((C, half), ab_ref[h], F32))
    pos = lax.broadcasted_iota(jnp.int32, (C, half), 0).astype(F32)
    tab_s[_QF] = jnp.exp(lg_f * (pos + 1.0))
    tab_s[_QB] = jnp.exp(lg_b * (C - pos))
    k_scale = RET_DH ** -0.5
    tab_s[_WF] = jnp.exp(lg_f * (C - 1.0 - pos)) * k_scale
    tab_s[_WB] = jnp.exp(lg_b * pos) * k_scale
    dec_f = jnp.exp(lg_f[:1] * C)
    dec_b = jnp.exp(lg_b[:1] * C)
    dec_f = jnp.concatenate([dec_f, dec_f], axis=1)
    dec_b = jnp.concatenate([dec_b, dec_b], axis=1)
    for c in range(C // half):
        row = lax.broadcasted_iota(jnp.int32, (C, half), 0)
        col = lax.broadcasted_iota(jnp.int32, (C, half), 1) + c * half
        lag = (row - col).astype(F32)
        alag = jnp.abs(lag)
        dm_s[:, c * half:(c + 1) * half] = jnp.where(
            lag >= 0, jnp.exp(lg_f * alag), jnp.exp(lg_b * alag)) * k_scale
    sf_s[...] = jnp.zeros_like(sf_s)
    sb_s[...] = jnp.zeros_like(sb_s)

    def rot(x, cos, sin):
        x1, x2 = x[:, :half], x[:, half:]
        return x1 * cos - x2 * sin, x2 * cos + x1 * sin

    def scaled16(x12, t):
        tab = tab_s[t]
        return jnp.concatenate([x12[0] * tab, x12[1] * tab], axis=1).astype(BF16)

    def update_state(s_ref, dec, k16, v):
        kv = lax.dot_general(k16, v, (((0,), (0,)), ((), ())), preferred_element_type=F32)
        s_ref[...] = s_ref[...] * dec + kv

    def first_visit(n, s_ref, dec, t_q, t_k, t_q_later, t_k_later):
        r = pl.ds(pl.multiple_of(n * C, C), C)
        cos = cos_ref[r, :]
        sin = sin_ref[r, :]
        q12 = rot(q_ref[r, :].astype(F32), cos, sin)
        k12 = rot(k_ref[r, :].astype(F32), cos, sin)
        v = v_ref[r, :]
        q16 = jnp.concatenate(q12, axis=1).astype(BF16)
        k16 = jnp.concatenate(k12, axis=1).astype(BF16)
        scores = lax.dot_general(q16, k16, (((1,), (1,)), ((), ())),
                                 preferred_element_type=F32) * dm_s[...]
        o = jnp.dot(scores.astype(BF16), v, preferred_element_type=F32)
        o = o + jnp.dot(scaled16(q12, t_q), s_ref[...].astype(BF16),
                        preferred_element_type=F32)
        acc_s[r, :] = o
        update_state(s_ref, dec, scaled16(k12, t_k), v)
        qx_s[r, :] = scaled16(q12, t_q_later)
        kx_s[r, :] = scaled16(k12, t_k_later)

    def second_visit(n, s_ref, dec):
        r = pl.ds(pl.multiple_of(n * C, C), C)
        o = acc_s[r, :] + jnp.dot(qx_s[r, :], s_ref[...].astype(BF16),
                                  preferred_element_type=F32)
        y = o * lax.rsqrt(jnp.mean(o * o, axis=-1, keepdims=True) + RMS_EPS)
        o_ref[r, :] = (y * _silu(g_ref[r, :].astype(F32))).astype(o_ref.dtype)
        update_state(s_ref, dec, kx_s[r, :], v_ref[r, :])

    def first_half(i, carry):
        first_visit(i, sf_s, dec_f, _QF, _WF, _QB, _WB)
        first_visit(n_chunks - 1 - i, sb_s, dec_b, _QB, _WB, _QF, _WF)
        return carry

    def second_half(i, carry):
        second_visit(i, sf_s, dec_f)
        second_visit(n_chunks - 1 - i, sb_s, dec_b)
        return carry

    lax.fori_loop(0, n_chunks // 2, first_half, 0, unroll=True)
    lax.fori_loop(n_chunks // 2, n_chunks, second_half, 0, unroll=True)


def _retention(zm, a_f, a_b, cos, sin, batch, seq):
    dh = RET_DH
    assert seq % (2 * RET_CHUNK) == 0
    blk = lambda off: pl.BlockSpec((seq, dh), lambda b, h, o=off // dh: (b, o + h))
    smem = pl.BlockSpec(memory_space=pltpu.SMEM)
    tab = pl.BlockSpec((seq, dh // 2), lambda b, h: (0, 0), pipeline_mode=pl.Buffered(1))
    return pl.pallas_call(
        functools.partial(_ret_kernel, seq=seq),
        grid=(batch, RET_HEADS),
        in_specs=[smem, smem, blk(ZM_RQ), blk(ZM_RK), blk(ZM_RV), blk(ZM_RG), tab, tab],
        out_specs=pl.BlockSpec((seq, dh), lambda b, h: (b, h)),
        out_shape=jax.ShapeDtypeStruct((batch * seq, BRANCH), BF16),
        scratch_shapes=[
            pltpu.VMEM((seq, dh), BF16),
            pltpu.VMEM((seq, dh), BF16),
            pltpu.VMEM((seq, dh), F32),
            pltpu.VMEM((dh, dh), F32),
            pltpu.VMEM((dh, dh), F32),
            pltpu.VMEM((RET_CHUNK, RET_CHUNK), F32),
            pltpu.VMEM((4, RET_CHUNK, dh // 2), F32),
        ],
        compiler_params=_GRID2,
        name="retention",
    )(a_f, a_b, zm, zm, zm, zm, cos, sin)


def _pool_kernel(hw_ref, u_ref, g_ref, cnt_ref, w_ref, sc_ref, o_ref, pad_s, band_s, *, seq, tile):
    M = POOL_MARGIN
    K = tile + 2 * M
    pair = 2 * tile
    hw = hw_ref[pl.program_id(1)]

    pad_s[0:M, :] = jnp.zeros((M, POOL_DG), BF16)
    pad_s[M + seq:M + seq + M, :] = jnp.zeros((M, POOL_DG), BF16)

    def fill(t, carry):
        r = pl.multiple_of(t * pair, pair)
        pad_s[pl.ds(M + r, pair), :] = u_ref[pl.ds(r, pair), :]
        return carry

    lax.fori_loop(0, seq // pair, fill, 0)

    d = (lax.broadcasted_iota(jnp.int32, (tile, K), 1)
         - lax.broadcasted_iota(jnp.int32, (tile, K), 0) - M)
    band_s[...] = jnp.where(d >= -hw, jnp.where(d < hw, 1.0, 0.0), 0.0).astype(BF16)
    w = w_ref[0]
    scale = sc_ref[...]

    def body(t, carry):
        r = pl.multiple_of(t * pair, pair)
        rows = pl.ds(r, pair)
        windows = jnp.concatenate([pad_s[pl.ds(r, K), :], pad_s[pl.ds(r + tile, K), :]], axis=1)
        wsum = jnp.dot(band_s[...], windows, preferred_element_type=F32)
        wsum = jnp.concatenate([wsum[:, :POOL_DG], wsum[:, POOL_DG:]], axis=0)
        cnt = cnt_ref[rows, :]
        cnt = jnp.concatenate([cnt] * (POOL_DG // cnt.shape[1]), axis=1)
        p = wsum / cnt - u_ref[rows, :].astype(F32)
        y = jnp.dot(p.astype(BF16), w, preferred_element_type=F32) * scale
        gate = _silu(g_ref[rows, :].astype(F32))
        o_ref[rows, :] = (y * gate).astype(o_ref.dtype)
        return carry

    lax.fori_loop(0, seq // pair, body, 0, unroll=True)


def _pool(zm, pool_w16, pool_scale, layer, batch, seq, tile=256):
    dg = POOL_DG
    n_groups = len(POOL_WINDOWS)
    assert max(POOL_WINDOWS) // 2 <= POOL_MARGIN and seq >= tile + 2 * POOL_MARGIN
    half_windows = jnp.asarray([w // 2 for w in POOL_WINDOWS], jnp.int32)
    pos = jnp.arange(seq, dtype=jnp.int32)[None, :, None]
    hw3 = half_windows[:, None, None]
    counts = (jnp.minimum(pos + hw3, seq) - jnp.maximum(pos - hw3, 0)).astype(F32)
    counts = jnp.broadcast_to(counts, (n_groups, seq, 128))
    return pl.pallas_call(
        functools.partial(_pool_kernel, seq=seq, tile=tile),
        grid=(batch, n_groups),
        in_specs=[
            pl.BlockSpec(memory_space=pltpu.SMEM),
            pl.BlockSpec((seq, dg), lambda b, g: (b, ZM_PV // dg + g)),
            pl.BlockSpec((seq, dg), lambda b, g: (b, ZM_PG // dg + g)),
            pl.BlockSpec((None, seq, 128), lambda b, g: (g, 0, 0)),
            pl.BlockSpec((None, 1, dg, dg), lambda b, g: (layer, g, 0, 0)),
            pl.BlockSpec((1, dg), lambda b, g: (0, g)),
        ],
        out_specs=pl.BlockSpec((seq, dg), lambda b, g: (b, g)),
        out_shape=jax.ShapeDtypeStruct((batch * seq, BRANCH), BF16),
        scratch_shapes=[pltpu.VMEM((seq + 2 * POOL_MARGIN, dg), BF16),
                        pltpu.VMEM((tile, tile + 2 * POOL_MARGIN), BF16)],
        compiler_params=_GRID2,
        name="pool",
    )(half_windows, zm, zm, counts, pool_w16, pool_scale)


def _att_kernel(sink_ref, q_ref, k_ref, v_ref, g_ref, o_ref, *, seq, nq):
    L = ATT_BLOCK
    G = ATT_GROUP
    kk = pl.program_id(1)
    n = pl.program_id(2)

    row = lax.broadcasted_iota(jnp.int32, (L, 3 * L), 0)
    col = lax.broadcasted_iota(jnp.int32, (L, 3 * L), 1)
    for blk in range(nq):
        nb = n * nq + blk
        rows = slice(blk * L, (blk + 1) * L)
        q = jnp.concatenate([q_ref[rows, h * ATT_DH:(h + 1) * ATT_DH] for h in range(G)],
                            axis=0)
        start = jnp.clip((nb - 1) * L, 0, seq - 3 * L)
        start = pl.multiple_of(start, L)
        kw = k_ref[pl.ds(start, 3 * L), :]
        vw = jnp.concatenate([v_ref[pl.ds(start, 3 * L), :], jnp.ones((3 * L, ATT_DH), BF16)],
                             axis=1)
        s = lax.dot_general(q, kw, (((1,), (1,)), ((), ())),
                            preferred_element_type=F32)
        valid = jnp.abs(col - row + (start - nb * L)) <= ATT_WINDOW
        outs = []
        for h in range(G):
            sh = jnp.where(valid, s[h * L:(h + 1) * L, :], NEG_BIG)
            sk = jnp.full((L, 1), sink_ref[kk * G + h] * LOG2_E, F32)
            m = jnp.maximum(jnp.max(sh, axis=-1, keepdims=True), sk)
            p = jnp.exp2(sh - m)
            pv = jnp.dot(p.astype(BF16), vw, preferred_element_type=F32)
            denom = pv[:, ATT_DH:] + jnp.exp2(sk - m)
            oh = pv[:, :ATT_DH] / denom
            gate = _silu(g_ref[rows, h * ATT_DH:(h + 1) * ATT_DH].astype(F32))
            outs.append((oh * gate).astype(o_ref.dtype))
        o_ref[rows, :] = jnp.concatenate(outs, axis=1)


def _attention(za, zm, sink, batch, seq, nq=32):
    L = ATT_BLOCK
    n_blocks = seq // L
    assert n_blocks % nq == 0
    n_steps = n_blocks // nq
    gw = ATT_GROUP * ATT_DH
    smem = pl.BlockSpec(memory_space=pltpu.SMEM)
    return pl.pallas_call(
        functools.partial(_att_kernel, seq=seq, nq=nq),
        grid=(batch, ATT_KV_HEADS, n_steps),
        in_specs=[
            smem,
            pl.BlockSpec((nq * L, gw), lambda b, k, n: (b * n_steps + n, k)),
            pl.BlockSpec((seq, ATT_DH), lambda b, k, n: (b, ZA_K // ATT_DH + k)),
            pl.BlockSpec((seq, ATT_DH), lambda b, k, n: (b, ZA_V // ATT_DH + k)),
            pl.BlockSpec((nq * L, gw), lambda b, k, n: (b * n_steps + n, ZM_AG // gw + k)),
        ],
        out_specs=pl.BlockSpec((nq * L, gw), lambda b, k, n: (b * n_steps + n, k)),
        out_shape=jax.ShapeDtypeStruct((batch * seq, BRANCH), BF16),
        compiler_params=pltpu.CompilerParams(
            dimension_semantics=("arbitrary", "arbitrary", "arbitrary"),
            vmem_limit_bytes=VMEM_LIMIT),
        name="attention",
    )(sink, za, za, za, zm)


def _merge_kernel(x_ref, ya_ref, yb_ref, yc_ref, *rest, n_col, emit_next):
    gate_refs = rest[:N_GATES * n_col]
    rest = rest[N_GATES * n_col:]
    if emit_next:
        wa_ref, wb_ref, wc_ref, wo_ref, gn_ref, o_ref, hn_ref, m_s = rest
    else:
        wa_ref, wb_ref, wc_ref, wo_ref, o_ref, m_s = rest
    tc = GATE_BLOCK
    ya = ya_ref[...]
    yb = yb_ref[...]
    yc = yc_ref[...]
    for c in range(n_col):
        cs = slice(c * tc, (c + 1) * tc)
        a = jnp.dot(ya, wa_ref[:, cs], preferred_element_type=F32)
        b = jnp.dot(yb, wb_ref[:, cs], preferred_element_type=F32)
        cc = jnp.dot(yc, wc_ref[:, cs], preferred_element_type=F32)
        g0, g1, g2 = (jax.nn.sigmoid(gate_refs[j * n_col + c][...].astype(F32))
                      for j in range(N_GATES))
        m_s[:, cs] = (g0 * a + g1 * b + g2 * cc).astype(BF16)
    x_new = x_ref[...] + jnp.dot(m_s[...], wo_ref[...], preferred_element_type=F32)
    o_ref[...] = x_new
    if emit_next:
        hn_ref[...] = _rms_norm_rows(x_new, gn_ref[...]).astype(hn_ref.dtype)


def _merge(x2, ya, yb, yc, zm, wa, wb, wc, wo, layer, g_next, tm=256):
    m, d = x2.shape
    n_col = d // GATE_BLOCK
    emit_next = g_next is not None
    row = lambda w: pl.BlockSpec((tm, w), lambda i: (i, 0))
    gate = lambda j, c: pl.BlockSpec(
        (tm, GATE_BLOCK), lambda i, o=ZM_MG // GATE_BLOCK + j * n_col + c: (i, o))
    gates = [gate(j, c) for j in range(N_GATES) for c in range(n_col)]
    wspec = lambda k: pl.BlockSpec((None, k, d), lambda i: (layer, 0, 0),
                                   pipeline_mode=pl.Buffered(1))
    in_specs = ([row(d), row(BRANCH), row(BRANCH), row(BRANCH)] + gates
                + [wspec(BRANCH), wspec(BRANCH), wspec(BRANCH), wspec(d)])
    args = [x2, ya, yb, yc] + [zm] * len(gates) + [wa, wb, wc, wo]
    out_specs = row(d)
    out_shape = jax.ShapeDtypeStruct((m, d), F32)
    if emit_next:
        in_specs.append(pl.BlockSpec((1, d), lambda i: (0, 0)))
        args.append(g_next)
        out_specs = (out_specs, row(d))
        out_shape = (out_shape, jax.ShapeDtypeStruct((m, d), BF16))
    return pl.pallas_call(
        functools.partial(_merge_kernel, n_col=n_col, emit_next=emit_next),
        grid=(m // tm,),
        in_specs=in_specs,
        out_specs=out_specs,
        out_shape=out_shape,
        scratch_shapes=[pltpu.VMEM((tm, d), BF16)],
        compiler_params=pltpu.CompilerParams(
            dimension_semantics=("arbitrary",),
            vmem_limit_bytes=VMEM_LIMIT),
        name="merge",
    )(*args)


def _ret_tables(seq):
    half = RET_DH // 2
    inv = 1.0 / (RET_ROPE_BASE ** jnp.linspace(0.0, 1.0, half, dtype=F32))
    ang = jnp.arange(seq, dtype=F32)[:, None] * inv[None, :]
    return jnp.cos(ang), jnp.sin(ang)


def _att_tables(seq):
    half = ROPE_DIMS // 2
    inv = ROPE_THETA ** (-jnp.arange(half, dtype=F32) / half)
    ang = jnp.arange(seq, dtype=F32)[:, None] * inv[None, :]
    c, s = jnp.cos(ang), jnp.sin(ang)
    rest = ATT_DH - ROPE_DIMS
    cos = jnp.concatenate([c, c, jnp.ones((seq, rest), F32)], axis=1)
    sin = jnp.concatenate([-s, s, jnp.zeros((seq, rest), F32)], axis=1)
    return cos, sin


def kernel(x, norm_g, w_in, ret_decay_fwd, ret_decay_bwd, pool_w, pool_scale,
           attn_q_gain, attn_k_gain, attn_sink, w_ret, w_pool, w_att, w_out):
    batch, seq, d = x.shape
    depth = w_in.shape[0]
    assert d == D_MODEL and w_in.shape[2] == IN_WIDTH
    assert seq >= 3 * ATT_BLOCK
    rcos, rsin = _ret_tables(seq)
    acos, asin = _att_tables(seq)
    w_ret16, w_pool16, w_att16, w_out16, pool_w16 = (
        w.astype(BF16) for w in (w_ret, w_pool, w_att, w_out, pool_w))
    x2 = x.reshape(batch * seq, d)
    h = _norm(x2, norm_g[0][None, :])
    for l in range(depth):
        q_gain = attn_q_gain[l] * (ATT_DH ** -0.5 * LOG2_E)
        qk_gains = jnp.concatenate(
            [jnp.tile(q_gain, ATT_Q_HEADS), jnp.tile(attn_k_gain[l], ATT_KV_HEADS),
             jnp.ones((ATT_KV_WIDTH,), F32)])[None, :]
        za = _proj_att(h, w_in, l, qk_gains, acos, asin, seq)
        zm = _proj_main(h, w_in, l)
        ya = _retention(zm, ret_decay_fwd[l], ret_decay_bwd[l], rcos, rsin, batch, seq)
        yb = _pool(zm, pool_w16, pool_scale[l][None, :], l, batch, seq)
        yc = _attention(za, zm, attn_sink[l], batch, seq)
        g_next = norm_g[l + 1][None, :] if l + 1 < depth else None
        res = _merge(x2, ya, yb, yc, zm, w_ret16, w_pool16, w_att16, w_out16, l, g_next)
        x2, h = res if g_next is not None else (res, None)
    return x2.reshape(batch, seq, d)
```

```python
import functools

import jax
import jax.numpy as jnp
from jax import lax
from jax.experimental import pallas as pl
from jax.experimental.pallas import tpu as pltpu

F32 = jnp.float32
BF16 = jnp.bfloat16

D_MODEL = 2048
BRANCH = D_MODEL // 2
RET_HEADS = 4
RET_DH = BRANCH // RET_HEADS
RET_CHUNK = 256
RET_ROPE_BASE = 10000.0
POOL_WINDOWS = (2, 4, 8, 16)
POOL_DG = BRANCH // len(POOL_WINDOWS)
POOL_MARGIN = 128
ATT_DH = 128
ATT_Q_HEADS = BRANCH // ATT_DH
ATT_KV_HEADS = ATT_Q_HEADS // 4
ATT_GROUP = ATT_Q_HEADS // ATT_KV_HEADS
ATT_KV_WIDTH = ATT_KV_HEADS * ATT_DH
ATT_BLOCK = 128
ATT_WINDOW = 128
ROPE_THETA = 500000.0
ROPE_DIMS = ATT_DH // 4
RMS_EPS = 1e-6
NEG_BIG = -1e30
LOG2_E = 1.4426950408889634
N_GATES = 3
MG_WIDTH = N_GATES * D_MODEL

W_AQ = 6 * BRANCH
W_AG = W_AQ + BRANCH + 2 * ATT_KV_WIDTH
IN_WIDTH = W_AG + BRANCH + MG_WIDTH

ZM_RQ, ZM_RK, ZM_RV, ZM_RG, ZM_PV, ZM_PG, ZM_AG, ZM_MG = (k * BRANCH for k in range(8))
ZM_WIDTH = ZM_MG + MG_WIDTH
ZA_WIDTH = BRANCH + 2 * ATT_KV_WIDTH
ZA_K = BRANCH
ZA_V = BRANCH + ATT_KV_WIDTH
GATE_BLOCK = 1024

VMEM_LIMIT = 56 * 1024 * 1024


def _silu(x):
    return x * jax.nn.sigmoid(x)


def _rms_norm_rows(x, gain):
    ms = jnp.mean(x * x, axis=-1, keepdims=True)
    return x * lax.rsqrt(ms + RMS_EPS) * gain


def _norm_kernel(x_ref, g_ref, h_ref):
    h_ref[...] = _rms_norm_rows(x_ref[...], g_ref[...]).astype(h_ref.dtype)


def _norm(x2, g, tm=512):
    m, d = x2.shape
    return pl.pallas_call(
        _norm_kernel,
        grid=(m // tm,),
        in_specs=[pl.BlockSpec((tm, d), lambda i: (i, 0)),
                  pl.BlockSpec((1, d), lambda i: (0, 0))],
        out_specs=pl.BlockSpec((tm, d), lambda i: (i, 0)),
        out_shape=jax.ShapeDtypeStruct((m, d), BF16),
        compiler_params=pltpu.CompilerParams(
            dimension_semantics=("arbitrary",), vmem_limit_bytes=VMEM_LIMIT),
        name="norm",
    )(x2, g)


_GRID2 = pltpu.CompilerParams(dimension_semantics=("arbitrary", "arbitrary"),
                              vmem_limit_bytes=VMEM_LIMIT)


def _proj_main_kernel(h_ref, *refs):
    w_refs, z_ref, w16_s = refs[:-2], refs[-2], refs[-1]
    tn = w_refs[0].shape[1]

    @pl.when(pl.program_id(1) == 0)
    def _():
        for k, w_ref in enumerate(w_refs):
            w16_s[k] = w_ref[...].astype(BF16)

    h = h_ref[...]
    for k in range(len(w_refs)):
        z_ref[:, k * tn:(k + 1) * tn] = jnp.dot(
            h, w16_s[k], preferred_element_type=F32).astype(z_ref.dtype)


def _proj_main(h, w_all, layer, tm=2048, tn=512, tiles_per_step=2):
    m, d = h.shape
    n_before = W_AQ // tn
    skip = (W_AG - W_AQ) // tn
    n_tiles = ZM_WIDTH // tn
    assert n_tiles % tiles_per_step == 0

    def w_spec(k):
        def index(j, i):
            tile = j * tiles_per_step + k
            return layer, 0, jnp.where(tile < n_before, tile, tile + skip)
        return pl.BlockSpec((None, d, tn), index)

    return pl.pallas_call(
        _proj_main_kernel,
        grid=(n_tiles // tiles_per_step, m // tm),
        in_specs=[pl.BlockSpec((tm, d), lambda j, i: (i, 0))]
                 + [w_spec(k) for k in range(tiles_per_step)],
        out_specs=pl.BlockSpec((tm, tiles_per_step * tn), lambda j, i: (i, j)),
        out_shape=jax.ShapeDtypeStruct((m, ZM_WIDTH), BF16),
        scratch_shapes=[pltpu.VMEM((tiles_per_step, d, tn), BF16)],
        compiler_params=_GRID2,
        name="proj_main",
    )(h, *([w_all] * tiles_per_step))


def _rope_partial(x, cos, sin_signed):
    lane = lax.broadcasted_iota(jnp.int32, x.shape, 1)
    up = pltpu.roll(x, ATT_DH - ROPE_DIMS // 2, 1)
    down = pltpu.roll(x, ROPE_DIMS // 2, 1)
    partner = jnp.where(lane < ROPE_DIMS // 2, up, down)
    return x * cos + partner * sin_signed


def _proj_att_kernel(h_ref, w_ref, gain_ref, cos_ref, sin_ref, z_ref, acc_s, w16_s, *,
                     n_row, heads_per_tile, n_qk_heads):
    t = pl.program_id(0)

    @pl.when(t == 0)
    def _():
        acc_s[...] = jnp.zeros_like(acc_s)

    @pl.when(t % n_row == 0)
    def _():
        w16_s[...] = w_ref[...].astype(BF16)

    first_head = (jnp.maximum(t - 1, 0) // n_row) * heads_per_tile
    prev = acc_s[...]
    acc_s[...] = jnp.dot(h_ref[...], w16_s[...], preferred_element_type=F32)
    cos = cos_ref[...]
    sin = sin_ref[...]
    outs = []
    for hh in range(heads_per_tile):
        cols = slice(hh * ATT_DH, (hh + 1) * ATT_DH)
        xh = prev[:, cols]
        qk = _rope_partial(_rms_norm_rows(xh, gain_ref[:, cols]), cos, sin)
        outs.append(jnp.where(first_head + hh < n_qk_heads, qk, xh))
    z_ref[...] = jnp.concatenate(outs, axis=1).astype(z_ref.dtype)


def _proj_att(h, w_all, layer, gains, cos, sin, seq, tm=2048, tn=512):
    m, d = h.shape
    n_row = m // tm
    n_tiles = n_row * (ZA_WIDTH // tn)
    pos_tiles = seq // tm

    def cur(t):
        tt = jnp.minimum(t, n_tiles - 1)
        return tt % n_row, tt // n_row

    def prev(t):
        tt = jnp.maximum(t - 1, 0)
        return tt % n_row, tt // n_row

    tab = pl.BlockSpec((tm, ATT_DH), lambda t: (prev(t)[0] % pos_tiles, 0))
    return pl.pallas_call(
        functools.partial(_proj_att_kernel, n_row=n_row, heads_per_tile=tn // ATT_DH,
                          n_qk_heads=ZA_V // ATT_DH),
        grid=(n_tiles + 1,),
        in_specs=[
            pl.BlockSpec((tm, d), lambda t: (cur(t)[0], 0)),
            pl.BlockSpec((None, d, tn), lambda t: (layer, 0, W_AQ // tn + cur(t)[1])),
            pl.BlockSpec((1, tn), lambda t: (0, prev(t)[1])),
            tab, tab,
        ],
        out_specs=pl.BlockSpec((tm, tn), lambda t: prev(t)),
        out_shape=jax.ShapeDtypeStruct((m, ZA_WIDTH), BF16),
        scratch_shapes=[pltpu.VMEM((tm, tn), F32), pltpu.VMEM((d, tn), BF16)],
        compiler_params=pltpu.CompilerParams(dimension_semantics=("arbitrary",),
                                             vmem_limit_bytes=VMEM_LIMIT),
        name="proj_att",
    )(h, w_all, gains, cos, sin)


_QF, _QB, _WF, _WB = range(4)


def _ret_kernel(af_ref, ab_ref, q_ref, k_ref, v_ref, g_ref, cos_ref, sin_ref,
                o_ref, qx_s, kx_s, acc_s, sf_s, sb_s, dm_s, tab_s, *, seq):
    C = RET_CHUNK
    n_chunks = seq // C
    half = RET_DH // 2
    h = pl.program_id(1)

    lg_f = -jnp.exp(jnp.full((C, half), af_ref[h], F32))
    lg_b = -jnp.exp(jnp.full((C, half), ab_ref[h], F32))
    pos = lax.broadcasted_iota(jnp.int32, (C, half), 0).astype(F32)
    tab_s[_QF] = jnp.exp(lg_f * (pos + 1.0))
    tab_s[_QB] = jnp.exp(lg_b * (C - pos))
    k_scale = RET_DH ** -0.5
    tab_s[_WF] = jnp.exp(lg_f * (C - 1.0 - pos)) * k_scale
    tab_s[_WB] = jnp.exp(lg_b * pos) * k_scale
    dec_f = jnp.exp(lg_f[:1] * C)
    dec_b = jnp.exp(lg_b[:1] * C)
    dec_f = jnp.concatenate([dec_f, dec_f], axis=1)
    dec_b = jnp.concatenate([dec_b, dec_b], axis=1)
    for c in range(C // half):
        row = lax.broadcasted_iota(jnp.int32, (C, half), 0)
        col = lax.broadcasted_iota(jnp.int32, (C, half), 1) + c * half
        lag = (row - col).astype(F32)
        alag = jnp.abs(lag)
        dm_s[:, c * half:(c + 1) * half] = jnp.where(
            lag >= 0, jnp.exp(lg_f * alag), jnp.exp(lg_b * alag)) * k_scale
    sf_s[...] = jnp.zeros_like(sf_s)
    sb_s[...] = jnp.zeros_like(sb_s)

    def rot(x, cos, sin):
        x1, x2 = x[:, :half], x[:, half:]
        return x1 * cos - x2 * sin, x2 * cos + x1 * sin

    def scaled16(x12, t):
        tab = tab_s[t]
        return jnp.concatenate([x12[0] * tab, x12[1] * tab], axis=1).astype(BF16)

    def update_state(s_ref, dec, k16, v):
        kv = lax.dot_general(k16, v, (((0,), (0,)), ((), ())), preferred_element_type=F32)
        s_ref[...] = s_ref[...] * dec + kv

    def first_visit(n, s_ref, dec, t_q, t_k, t_q_later, t_k_later):
        r = pl.ds(pl.multiple_of(n * C, C), C)
        cos = cos_ref[r, :]
        sin = sin_ref[r, :]
        q12 = rot(q_ref[r, :].astype(F32), cos, sin)
        k12 = rot(k_ref[r, :].astype(F32), cos, sin)
        v = v_ref[r, :]
        q16 = jnp.concatenate(q12, axis=1).astype(BF16)
        k16 = jnp.concatenate(k12, axis=1).astype(BF16)
        scores = lax.dot_general(q16, k16, (((1,), (1,)), ((), ())),
                                 preferred_element_type=F32) * dm_s[...]
        o = jnp.dot(scores.astype(BF16), v, preferred_element_type=F32)
        o = o + jnp.dot(scaled16(q12, t_q), s_ref[...].astype(BF16),
                        preferred_element_type=F32)
        acc_s[r, :] = o
        update_state(s_ref, dec, scaled16(k12, t_k), v)
        qx_s[r, :] = scaled16(q12, t_q_later)
        kx_s[r, :] = scaled16(k12, t_k_later)

    def second_visit(n, s_ref, dec):
        r = pl.ds(pl.multiple_of(n * C, C), C)
        o = acc_s[r, :] + jnp.dot(qx_s[r, :], s_ref[...].astype(BF16),
                                  preferred_element_type=F32)
        y = o * lax.rsqrt(jnp.mean(o * o, axis=-1, keepdims=True) + RMS_EPS)
        o_ref[r, :] = (y * _silu(g_ref[r, :].astype(F32))).astype(o_ref.dtype)
        update_state(s_ref, dec, kx_s[r, :], v_ref[r, :])

    def first_half(i, carry):
        first_visit(i, sf_s, dec_f, _QF, _WF, _QB, _WB)
        first_visit(n_chunks - 1 - i, sb_s, dec_b, _QB, _WB, _QF, _WF)
        return carry

    def second_half(i, carry):
        second_visit(i, sf_s, dec_f)
        second_visit(n_chunks - 1 - i, sb_s, dec_b)
        return carry

    lax.fori_loop(0, n_chunks // 2, first_half, 0, unroll=True)
    lax.fori_loop(n_chunks // 2, n_chunks, second_half, 0, unroll=True)


def _retention(zm, a_f, a_b, cos, sin, batch, seq):
    dh = RET_DH
    assert seq % (2 * RET_CHUNK) == 0
    blk = lambda off: pl.BlockSpec((seq, dh), lambda b, h, o=off // dh: (b, o + h))
    smem = pl.BlockSpec(memory_space=pltpu.SMEM)
    tab = pl.BlockSpec((seq, dh // 2), lambda b, h: (0, 0), pipeline_mode=pl.Buffered(1))
    return pl.pallas_call(
        functools.partial(_ret_kernel, seq=seq),
        grid=(batch, RET_HEADS),
        in_specs=[smem, smem, blk(ZM_RQ), blk(ZM_RK), blk(ZM_RV), blk(ZM_RG), tab, tab],
        out_specs=pl.BlockSpec((seq, dh), lambda b, h: (b, h)),
        out_shape=jax.ShapeDtypeStruct((batch * seq, BRANCH), BF16),
        scratch_shapes=[
            pltpu.VMEM((seq, dh), BF16),
            pltpu.VMEM((seq, dh), BF16),
            pltpu.VMEM((seq, dh), F32),
            pltpu.VMEM((dh, dh), F32),
            pltpu.VMEM((dh, dh), F32),
            pltpu.VMEM((RET_CHUNK, RET_CHUNK), F32),
            pltpu.VMEM((4, RET_CHUNK, dh // 2), F32),
        ],
        compiler_params=_GRID2,
        name="retention",
    )(a_f, a_b, zm, zm, zm, zm, cos, sin)


def _pool_kernel(hw_ref, u_ref, g_ref, cnt_ref, w_ref, sc_ref, o_ref, pad_s, band_s, *, seq, tile):
    M = POOL_MARGIN
    K = tile + 2 * M
    pair = 2 * tile
    hw = hw_ref[pl.program_id(0)]

    pad_s[0:M, :] = jnp.zeros((M, POOL_DG), BF16)
    pad_s[M + seq:M + seq + M, :] = jnp.zeros((M, POOL_DG), BF16)

    def fill(t, carry):
        r = pl.multiple_of(t * pair, pair)
        pad_s[pl.ds(M + r, pair), :] = u_ref[pl.ds(r, pair), :]
        return carry

    lax.fori_loop(0, seq // pair, fill, 0)

    @pl.when(pl.program_id(1) == 0)
    def _():
        d = (lax.broadcasted_iota(jnp.int32, (tile, K), 1)
             - lax.broadcasted_iota(jnp.int32, (tile, K), 0) - M)
        band_s[...] = jnp.where(d >= -hw, jnp.where(d < hw, 1.0, 0.0), 0.0).astype(BF16)

    w = w_ref[0]
    scale = sc_ref[...]

    def body(t, carry):
        r = pl.multiple_of(t * pair, pair)
        rows = pl.ds(r, pair)
        windows = jnp.concatenate([pad_s[pl.ds(r, K), :], pad_s[pl.ds(r + tile, K), :]], axis=1)
        wsum = jnp.dot(band_s[...], windows, preferred_element_type=F32)
        wsum = jnp.concatenate([wsum[:, :POOL_DG], wsum[:, POOL_DG:]], axis=0)
        cnt = cnt_ref[rows, :]
        cnt = jnp.concatenate([cnt] * (POOL_DG // cnt.shape[1]), axis=1)
        p = wsum / cnt - u_ref[rows, :].astype(F32)
        y = jnp.dot(p.astype(BF16), w, preferred_element_type=F32) * scale
        gate = _silu(g_ref[rows, :].astype(F32))
        o_ref[rows, :] = (y * gate).astype(o_ref.dtype)
        return carry

    lax.fori_loop(0, seq // pair, body, 0, unroll=True)


def _pool(zm, pool_w16, pool_scale, layer, batch, seq, tile=256):
    dg = POOL_DG
    n_groups = len(POOL_WINDOWS)
    assert max(POOL_WINDOWS) // 2 <= POOL_MARGIN and seq >= tile + 2 * POOL_MARGIN
    half_windows = jnp.asarray([w // 2 for w in POOL_WINDOWS], jnp.int32)
    pos = jnp.arange(seq, dtype=jnp.int32)[None, :, None]
    hw3 = half_windows[:, None, None]
    counts = (jnp.minimum(pos + hw3, seq) - jnp.maximum(pos - hw3, 0)).astype(F32)
    counts = jnp.broadcast_to(counts, (n_groups, seq, 128))
    return pl.pallas_call(
        functools.partial(_pool_kernel, seq=seq, tile=tile),
        grid=(n_groups, batch),
        in_specs=[
            pl.BlockSpec(memory_space=pltpu.SMEM),
            pl.BlockSpec((seq, dg), lambda g, b: (b, ZM_PV // dg + g)),
            pl.BlockSpec((seq, dg), lambda g, b: (b, ZM_PG // dg + g)),
            pl.BlockSpec((None, seq, 128), lambda g, b: (g, 0, 0)),
            pl.BlockSpec((None, 1, dg, dg), lambda g, b: (layer, g, 0, 0)),
            pl.BlockSpec((1, dg), lambda g, b: (0, g)),
        ],
        out_specs=pl.BlockSpec((seq, dg), lambda g, b: (b, g)),
        out_shape=jax.ShapeDtypeStruct((batch * seq, BRANCH), BF16),
        scratch_shapes=[pltpu.VMEM((seq + 2 * POOL_MARGIN, dg), BF16),
                        pltpu.VMEM((tile, tile + 2 * POOL_MARGIN), BF16)],
        compiler_params=_GRID2,
        name="pool",
    )(half_windows, zm, zm, counts, pool_w16, pool_scale)


def _att_kernel(sink_ref, q_ref, k_ref, v_ref, g_ref, o_ref, *, seq, nq):
    L = ATT_BLOCK
    G = ATT_GROUP
    kk = pl.program_id(1)
    n = pl.program_id(2)

    row = lax.broadcasted_iota(jnp.int32, (L, 3 * L), 0)
    col = lax.broadcasted_iota(jnp.int32, (L, 3 * L), 1)
    for blk in range(nq):
        nb = n * nq + blk
        rows = slice(blk * L, (blk + 1) * L)
        q = jnp.concatenate([q_ref[rows, h * ATT_DH:(h + 1) * ATT_DH] for h in range(G)],
                            axis=0)
        start = jnp.clip((nb - 1) * L, 0, seq - 3 * L)
        start = pl.multiple_of(start, L)
        kw = k_ref[pl.ds(start, 3 * L), :]
        vw = jnp.concatenate([v_ref[pl.ds(start, 3 * L), :], jnp.ones((3 * L, ATT_DH), BF16)],
                             axis=1)
        s = lax.dot_general(q, kw, (((1,), (1,)), ((), ())),
                            preferred_element_type=F32)
        valid = jnp.abs(col - row + (start - nb * L)) <= ATT_WINDOW
        outs = []
        for h in range(G):
            sh = jnp.where(valid, s[h * L:(h + 1) * L, :], NEG_BIG)
            sk = jnp.full((L, 1), sink_ref[kk * G + h] * LOG2_E, F32)
            m = jnp.maximum(jnp.max(sh, axis=-1, keepdims=True), sk)
            p = jnp.exp2(sh - m)
            pv = jnp.dot(p.astype(BF16), vw, preferred_element_type=F32)
            denom = pv[:, ATT_DH:] + jnp.exp2(sk - m)
            oh = pv[:, :ATT_DH] / denom
            gate = _silu(g_ref[rows, h * ATT_DH:(h + 1) * ATT_DH].astype(F32))
            outs.append((oh * gate).astype(o_ref.dtype))
        o_ref[rows, :] = jnp.concatenate(outs, axis=1)


def _attention(za, zm, sink, batch, seq, nq=32):
    L = ATT_BLOCK
    n_blocks = seq // L
    assert n_blocks % nq == 0
    n_steps = n_blocks // nq
    gw = ATT_GROUP * ATT_DH
    smem = pl.BlockSpec(memory_space=pltpu.SMEM)
    return pl.pallas_call(
        functools.partial(_att_kernel, seq=seq, nq=nq),
        grid=(batch, ATT_KV_HEADS, n_steps),
        in_specs=[
            smem,
            pl.BlockSpec((nq * L, gw), lambda b, k, n: (b * n_steps + n, k)),
            pl.BlockSpec((seq, ATT_DH), lambda b, k, n: (b, ZA_K // ATT_DH + k)),
            pl.BlockSpec((seq, ATT_DH), lambda b, k, n: (b, ZA_V // ATT_DH + k)),
            pl.BlockSpec((nq * L, gw), lambda b, k, n: (b * n_steps + n, ZM_AG // gw + k)),
        ],
        out_specs=pl.BlockSpec((nq * L, gw), lambda b, k, n: (b * n_steps + n, k)),
        out_shape=jax.ShapeDtypeStruct((batch * seq, BRANCH), BF16),
        compiler_params=pltpu.CompilerParams(
            dimension_semantics=("arbitrary", "arbitrary", "arbitrary"),
            vmem_limit_bytes=VMEM_LIMIT),
        name="attention",
    )(sink, za, za, za, zm)


def _merge_kernel(x_ref, ya_ref, yb_ref, yc_ref, *rest, n_col, emit_next):
    gate_refs = rest[:N_GATES * n_col]
    rest = rest[N_GATES * n_col:]
    if emit_next:
        wa_ref, wb_ref, wc_ref, wo_ref, gn_ref, o_ref, hn_ref, m_s = rest
    else:
        wa_ref, wb_ref, wc_ref, wo_ref, o_ref, m_s = rest
    tc = GATE_BLOCK
    ya = ya_ref[...]
    yb = yb_ref[...]
    yc = yc_ref[...]
    for c in range(n_col):
        cs = slice(c * tc, (c + 1) * tc)
        a = jnp.dot(ya, wa_ref[:, cs], preferred_element_type=F32)
        b = jnp.dot(yb, wb_ref[:, cs], preferred_element_type=F32)
        cc = jnp.dot(yc, wc_ref[:, cs], preferred_element_type=F32)
        g0, g1, g2 = (jax.nn.sigmoid(gate_refs[j * n_col + c][...].astype(F32))
                      for j in range(N_GATES))
        m_s[:, cs] = (g0 * a + g1 * b + g2 * cc).astype(BF16)
    x_new = x_ref[...] + jnp.dot(m_s[...], wo_ref[...], preferred_element_type=F32)
    o_ref[...] = x_new
    if emit_next:
        hn_ref[...] = _rms_norm_rows(x_new, gn_ref[...]).astype(hn_ref.dtype)


def _merge(x2, ya, yb, yc, zm, wa, wb, wc, wo, layer, g_next, tm=256):
    m, d = x2.shape
    n_col = d // GATE_BLOCK
    emit_next = g_next is not None
    row = lambda w: pl.BlockSpec((tm, w), lambda i: (i, 0))
    gate = lambda j, c: pl.BlockSpec(
        (tm, GATE_BLOCK), lambda i, o=ZM_MG // GATE_BLOCK + j * n_col + c: (i, o))
    gates = [gate(j, c) for j in range(N_GATES) for c in range(n_col)]
    wspec = lambda k: pl.BlockSpec((None, k, d), lambda i: (layer, 0, 0),
                                   pipeline_mode=pl.Buffered(1))
    in_specs = ([row(d), row(BRANCH), row(BRANCH), row(BRANCH)] + gates
                + [wspec(BRANCH), wspec(BRANCH), wspec(BRANCH), wspec(d)])
    args = [x2, ya, yb, yc] + [zm] * len(gates) + [wa, wb, wc, wo]
    out_specs = row(d)
    out_shape = jax.ShapeDtypeStruct((m, d), F32)
    if emit_next:
        in_specs.append(pl.BlockSpec((1, d), lambda i: (0, 0)))
        args.append(g_next)
        out_specs = (out_specs, row(d))
        out_shape = (out_shape, jax.ShapeDtypeStruct((m, d), BF16))
    return pl.pallas_call(
        functools.partial(_merge_kernel, n_col=n_col, emit_next=emit_next),
        grid=(m // tm,),
        in_specs=in_specs,
        out_specs=out_specs,
        out_shape=out_shape,
        scratch_shapes=[pltpu.VMEM((tm, d), BF16)],
        compiler_params=pltpu.CompilerParams(
            dimension_semantics=("arbitrary",),
            vmem_limit_bytes=VMEM_LIMIT),
        name="merge",
    )(*args)


def _ret_tables(seq):
    half = RET_DH // 2
    inv = 1.0 / (RET_ROPE_BASE ** jnp.linspace(0.0, 1.0, half, dtype=F32))
    ang = jnp.arange(seq, dtype=F32)[:, None] * inv[None, :]
    return jnp.cos(ang), jnp.sin(ang)


def _att_tables(seq):
    half = ROPE_DIMS // 2
    inv = ROPE_THETA ** (-jnp.arange(half, dtype=F32) / half)
    ang = jnp.arange(seq, dtype=F32)[:, None] * inv[None, :]
    c, s = jnp.cos(ang), jnp.sin(ang)
    rest = ATT_DH - ROPE_DIMS
    cos = jnp.concatenate([c, c, jnp.ones((seq, rest), F32)], axis=1)
    sin = jnp.concatenate([-s, s, jnp.zeros((seq, rest), F32)], axis=1)
    return cos, sin


def kernel(x, norm_g, w_in, ret_decay_fwd, ret_decay_bwd, pool_w, pool_scale,
           attn_q_gain, attn_k_gain, attn_sink, w_ret, w_pool, w_att, w_out):
    batch, seq, d = x.shape
    depth = w_in.shape[0]
    assert d == D_MODEL and w_in.shape[2] == IN_WIDTH
    assert seq >= 3 * ATT_BLOCK
    rcos, rsin = _ret_tables(seq)
    acos, asin = _att_tables(seq)
    w_ret16, w_pool16, w_att16, w_out16, pool_w16 = (
        w.astype(BF16) for w in (w_ret, w_pool, w_att, w_out, pool_w))
    x2 = x.reshape(batch * seq, d)
    h = _norm(x2, norm_g[0][None, :])
    for l in range(depth):
        q_gain = attn_q_gain[l] * (ATT_DH ** -0.5 * LOG2_E)
        qk_gains = jnp.concatenate(
            [jnp.tile(q_gain, ATT_Q_HEADS), jnp.tile(attn_k_gain[l], ATT_KV_HEADS),
             jnp.ones((ATT_KV_WIDTH,), F32)])[None, :]
        za = _proj_att(h, w_in, l, qk_gains, acos, asin, seq)
        zm = _proj_main(h, w_in, l)
        ya = _retention(zm, ret_decay_fwd[l], ret_decay_bwd[l], rcos, rsin, batch, seq)
        yb = _pool(zm, pool_w16, pool_scale[l][None, :], l, batch, seq)
        yc = _attention(za, zm, attn_sink[l], batch, seq)
        g_next = norm_g[l + 1][None, :] if l + 1 < depth else None
        res = _merge(x2, ya, yb, yc, zm, w_ret16, w_pool16, w_att16, w_out16, l, g_next)
        x2, h = res if g_next is not None else (res, None)
    return x2.reshape(batch, seq, d)
```

```python
import functools

import jax
import jax.numpy as jnp
from jax import lax
from jax.experimental import pallas as pl
from jax.experimental.pallas import tpu as pltpu

F32 = jnp.float32
BF16 = jnp.bfloat16

D_MODEL = 2048
BRANCH = D_MODEL // 2
RET_HEADS = 4
RET_DH = BRANCH // RET_HEADS
RET_CHUNK = 256
RET_ROPE_BASE = 10000.0
POOL_WINDOWS = (2, 4, 8, 16)
POOL_DG = BRANCH // len(POOL_WINDOWS)
POOL_MARGIN = 128
ATT_DH = 128
ATT_Q_HEADS = BRANCH // ATT_DH
ATT_KV_HEADS = ATT_Q_HEADS // 4
ATT_GROUP = ATT_Q_HEADS // ATT_KV_HEADS
ATT_KV_WIDTH = ATT_KV_HEADS * ATT_DH
ATT_BLOCK = 128
ATT_WINDOW = 128
ROPE_THETA = 500000.0
ROPE_DIMS = ATT_DH // 4
RMS_EPS = 1e-6
NEG_BIG = -1e30
LOG2_E = 1.4426950408889634
N_GATES = 3
MG_WIDTH = N_GATES * D_MODEL

W_AQ = 6 * BRANCH
W_AG = W_AQ + BRANCH + 2 * ATT_KV_WIDTH
IN_WIDTH = W_AG + BRANCH + MG_WIDTH

ZM_RQ, ZM_RK, ZM_RV, ZM_RG, ZM_PV, ZM_PG, ZM_AG, ZM_MG = (k * BRANCH for k in range(8))
ZM_WIDTH = ZM_MG + MG_WIDTH
ZA_WIDTH = BRANCH + 2 * ATT_KV_WIDTH
ZA_K = BRANCH
ZA_V = BRANCH + ATT_KV_WIDTH
GATE_BLOCK = 1024
EPILOGUE_ROW_CHUNKS = 4

VMEM_LIMIT = 56 * 1024 * 1024


def _silu(x):
    return x * jax.nn.sigmoid(x)


def _rms_norm_rows(x, gain):
    ms = jnp.mean(x * x, axis=-1, keepdims=True)
    return x * lax.rsqrt(ms + RMS_EPS) * gain


def _norm_kernel(x_ref, g_ref, h_ref):
    h_ref[...] = _rms_norm_rows(x_ref[...], g_ref[...]).astype(h_ref.dtype)


def _norm(x2, g, tm=512):
    m, d = x2.shape
    return pl.pallas_call(
        _norm_kernel,
        grid=(m // tm,),
        in_specs=[pl.BlockSpec((tm, d), lambda i: (i, 0)),
                  pl.BlockSpec((1, d), lambda i: (0, 0))],
        out_specs=pl.BlockSpec((tm, d), lambda i: (i, 0)),
        out_shape=jax.ShapeDtypeStruct((m, d), BF16),
        compiler_params=pltpu.CompilerParams(
            dimension_semantics=("arbitrary",), vmem_limit_bytes=VMEM_LIMIT),
        name="norm",
    )(x2, g)


_GRID2 = pltpu.CompilerParams(dimension_semantics=("arbitrary", "arbitrary"),
                              vmem_limit_bytes=VMEM_LIMIT)


def _proj_main_kernel(h_ref, *refs):
    w_refs, z_ref, w16_s = refs[:-2], refs[-2], refs[-1]
    tn = w_refs[0].shape[1]

    @pl.when(pl.program_id(1) == 0)
    def _():
        for k, w_ref in enumerate(w_refs):
            w16_s[k] = w_ref[...].astype(BF16)

    h = h_ref[...]
    for k in range(len(w_refs)):
        z_ref[:, k * tn:(k + 1) * tn] = jnp.dot(
            h, w16_s[k], preferred_element_type=F32).astype(z_ref.dtype)


def _proj_main(h, w_all, layer, tm=2048, tn=512, tiles_per_step=2):
    m, d = h.shape
    n_before = W_AQ // tn
    skip = (W_AG - W_AQ) // tn
    n_tiles = ZM_WIDTH // tn
    assert n_tiles % tiles_per_step == 0

    def w_spec(k):
        def index(j, i):
            tile = j * tiles_per_step + k
            return layer, 0, jnp.where(tile < n_before, tile, tile + skip)
        return pl.BlockSpec((None, d, tn), index)

    return pl.pallas_call(
        _proj_main_kernel,
        grid=(n_tiles // tiles_per_step, m // tm),
        in_specs=[pl.BlockSpec((tm, d), lambda j, i: (i, 0))]
                 + [w_spec(k) for k in range(tiles_per_step)],
        out_specs=pl.BlockSpec((tm, tiles_per_step * tn), lambda j, i: (i, j)),
        out_shape=jax.ShapeDtypeStruct((m, ZM_WIDTH), BF16),
        scratch_shapes=[pltpu.VMEM((tiles_per_step, d, tn), BF16)],
        compiler_params=_GRID2,
        name="proj_main",
    )(h, *([w_all] * tiles_per_step))


def _rope_partial(x, cos, sin_signed):
    lane = lax.broadcasted_iota(jnp.int32, x.shape, 1)
    up = pltpu.roll(x, ATT_DH - ROPE_DIMS // 2, 1)
    down = pltpu.roll(x, ROPE_DIMS // 2, 1)
    partner = jnp.where(lane < ROPE_DIMS // 2, up, down)
    return x * cos + partner * sin_signed


def _proj_att_kernel(h_ref, w_ref, gain_ref, cos_ref, sin_ref, z_ref, acc_s, w16_s, *,
                     n_row, heads_per_tile, n_qk_heads):
    t = pl.program_id(0)

    @pl.when(t == 0)
    def _():
        acc_s[...] = jnp.zeros_like(acc_s)

    @pl.when(t % n_row == 0)
    def _():
        w16_s[...] = w_ref[...].astype(BF16)

    first_head = (jnp.maximum(t - 1, 0) // n_row) * heads_per_tile
    tm = acc_s.shape[0]
    for rc in range(EPILOGUE_ROW_CHUNKS):
        rows = slice(rc * tm // EPILOGUE_ROW_CHUNKS, (rc + 1) * tm // EPILOGUE_ROW_CHUNKS)
        cos = cos_ref[rows, :]
        sin = sin_ref[rows, :]
        for hh in range(heads_per_tile):
            cols = slice(hh * ATT_DH, (hh + 1) * ATT_DH)
            xh = acc_s[rows, cols]
            qk = _rope_partial(_rms_norm_rows(xh, gain_ref[:, cols]), cos, sin)
            z_ref[rows, cols] = jnp.where(first_head + hh < n_qk_heads, qk, xh).astype(z_ref.dtype)
    acc_s[...] = jnp.dot(h_ref[...], w16_s[...], preferred_element_type=F32)


def _proj_att(h, w_all, layer, gains, cos, sin, seq, tm=2048, tn=512):
    m, d = h.shape
    n_row = m // tm
    n_tiles = n_row * (ZA_WIDTH // tn)
    pos_tiles = seq // tm

    def cur(t):
        tt = jnp.minimum(t, n_tiles - 1)
        return tt % n_row, tt // n_row

    def prev(t):
        tt = jnp.maximum(t - 1, 0)
        return tt % n_row, tt // n_row

    tab = pl.BlockSpec((tm, ATT_DH), lambda t: (prev(t)[0] % pos_tiles, 0))
    return pl.pallas_call(
        functools.partial(_proj_att_kernel, n_row=n_row, heads_per_tile=tn // ATT_DH,
                          n_qk_heads=ZA_V // ATT_DH),
        grid=(n_tiles + 1,),
        in_specs=[
            pl.BlockSpec((tm, d), lambda t: (cur(t)[0], 0)),
            pl.BlockSpec((None, d, tn), lambda t: (layer, 0, W_AQ // tn + cur(t)[1])),
            pl.BlockSpec((1, tn), lambda t: (0, prev(t)[1])),
            tab, tab,
        ],
        out_specs=pl.BlockSpec((tm, tn), lambda t: prev(t)),
        out_shape=jax.ShapeDtypeStruct((m, ZA_WIDTH), BF16),
        scratch_shapes=[pltpu.VMEM((tm, tn), F32), pltpu.VMEM((d, tn), BF16)],
        compiler_params=pltpu.CompilerParams(dimension_semantics=("arbitrary",),
                                             vmem_limit_bytes=VMEM_LIMIT),
        name="proj_att",
    )(h, w_all, gains, cos, sin)


_QF, _QB, _WF, _WB = range(4)


def _ret_kernel(af_ref, ab_ref, q_ref, k_ref, v_ref, g_ref, cos_ref, sin_ref,
                o_ref, qx_s, kx_s, acc_s, sf_s, sb_s, dm_s, tab_s, *, seq):
    C = RET_CHUNK
    n_chunks = seq // C
    half = RET_DH // 2
    h = pl.program_id(1)

    lg_f = -jnp.exp(jnp.full((C, half), af_ref[h], F32))
    lg_b = -jnp.exp(jnp.full((C, half), ab_ref[h], F32))
    pos = lax.broadcasted_iota(jnp.int32, (C, half), 0).astype(F32)
    tab_s[_QF] = jnp.exp(lg_f * (pos + 1.0))
    tab_s[_QB] = jnp.exp(lg_b * (C - pos))
    k_scale = RET_DH ** -0.5
    tab_s[_WF] = jnp.exp(lg_f * (C - 1.0 - pos)) * k_scale
    tab_s[_WB] = jnp.exp(lg_b * pos) * k_scale
    dec_f = jnp.exp(lg_f[:1] * C)
    dec_b = jnp.exp(lg_b[:1] * C)
    dec_f = jnp.concatenate([dec_f, dec_f], axis=1)
    dec_b = jnp.concatenate([dec_b, dec_b], axis=1)
    for c in range(C // half):
        row = lax.broadcasted_iota(jnp.int32, (C, half), 0)
        col = lax.broadcasted_iota(jnp.int32, (C, half), 1) + c * half
        lag = (row - col).astype(F32)
        alag = jnp.abs(lag)
        dm_s[:, c * half:(c + 1) * half] = jnp.where(
            lag >= 0, jnp.exp(lg_f * alag), jnp.exp(lg_b * alag)) * k_scale
    sf_s[...] = jnp.zeros_like(sf_s)
    sb_s[...] = jnp.zeros_like(sb_s)

    def rot(x, cos, sin):
        x1, x2 = x[:, :half], x[:, half:]
        return x1 * cos - x2 * sin, x2 * cos + x1 * sin

    def scaled16(x12, t):
        tab = tab_s[t]
        return jnp.concatenate([x12[0] * tab, x12[1] * tab], axis=1).astype(BF16)

    def update_state(s_ref, dec, k16, v):
        kv = lax.dot_general(k16, v, (((0,), (0,)), ((), ())), preferred_element_type=F32)
        s_ref[...] = s_ref[...] * dec + kv

    def first_visit(n, s_ref, dec, t_q, t_k, t_q_later, t_k_later):
        r = pl.ds(pl.multiple_of(n * C, C), C)
        cos = cos_ref[r, :]
        sin = sin_ref[r, :]
        q12 = rot(q_ref[r, :].astype(F32), cos, sin)
        k12 = rot(k_ref[r, :].astype(F32), cos, sin)
        v = v_ref[r, :]
        q16 = jnp.concatenate(q12, axis=1).astype(BF16)
        k16 = jnp.concatenate(k12, axis=1).astype(BF16)
        scores = lax.dot_general(q16, k16, (((1,), (1,)), ((), ())),
                                 preferred_element_type=F32) * dm_s[...]
        o = jnp.dot(scores.astype(BF16), v, preferred_element_type=F32)
        o = o + jnp.dot(scaled16(q12, t_q), s_ref[...].astype(BF16),
                        preferred_element_type=F32)
        acc_s[r, :] = o
        update_state(s_ref, dec, scaled16(k12, t_k), v)
        qx_s[r, :] = scaled16(q12, t_q_later)
        kx_s[r, :] = scaled16(k12, t_k_later)

    def second_visit(n, s_ref, dec):
        r = pl.ds(pl.multiple_of(n * C, C), C)
        o = acc_s[r, :] + jnp.dot(qx_s[r, :], s_ref[...].astype(BF16),
                                  preferred_element_type=F32)
        y = o * lax.rsqrt(jnp.mean(o * o, axis=-1, keepdims=True) + RMS_EPS)
        o_ref[r, :] = (y * _silu(g_ref[r, :].astype(F32))).astype(o_ref.dtype)
        update_state(s_ref, dec, kx_s[r, :], v_ref[r, :])

    def first_half(i, carry):
        first_visit(i, sf_s, dec_f, _QF, _WF, _QB, _WB)
        first_visit(n_chunks - 1 - i, sb_s, dec_b, _QB, _WB, _QF, _WF)
        return carry

    def second_half(i, carry):
        second_visit(i, sf_s, dec_f)
        second_visit(n_chunks - 1 - i, sb_s, dec_b)
        return carry

    lax.fori_loop(0, n_chunks // 2, first_half, 0, unroll=True)
    lax.fori_loop(n_chunks // 2, n_chunks, second_half, 0, unroll=True)


def _retention(zm, a_f, a_b, cos, sin, batch, seq):
    dh = RET_DH
    assert seq % (2 * RET_CHUNK) == 0
    blk = lambda off: pl.BlockSpec((seq, dh), lambda b, h, o=off // dh: (b, o + h))
    smem = pl.BlockSpec(memory_space=pltpu.SMEM)
    tab = pl.BlockSpec((seq, dh // 2), lambda b, h: (0, 0), pipeline_mode=pl.Buffered(1))
    return pl.pallas_call(
        functools.partial(_ret_kernel, seq=seq),
        grid=(batch, RET_HEADS),
        in_specs=[smem, smem, blk(ZM_RQ), blk(ZM_RK), blk(ZM_RV), blk(ZM_RG), tab, tab],
        out_specs=pl.BlockSpec((seq, dh), lambda b, h: (b, h)),
        out_shape=jax.ShapeDtypeStruct((batch * seq, BRANCH), BF16),
        scratch_shapes=[
            pltpu.VMEM((seq, dh), BF16),
            pltpu.VMEM((seq, dh), BF16),
            pltpu.VMEM((seq, dh), F32),
            pltpu.VMEM((dh, dh), F32),
            pltpu.VMEM((dh, dh), F32),
            pltpu.VMEM((RET_CHUNK, RET_CHUNK), F32),
            pltpu.VMEM((4, RET_CHUNK, dh // 2), F32),
        ],
        compiler_params=_GRID2,
        name="retention",
    )(a_f, a_b, zm, zm, zm, zm, cos, sin)


def _pool_kernel(hw_ref, u_ref, g_ref, cnt_ref, w_ref, sc_ref, o_ref, pad_s, band_s, *, seq, tile):
    M = POOL_MARGIN
    K = tile + 2 * M
    pair = 2 * tile
    hw = hw_ref[pl.program_id(0)]

    pad_s[0:M, :] = jnp.zeros((M, POOL_DG), BF16)
    pad_s[M + seq:M + seq + M, :] = jnp.zeros((M, POOL_DG), BF16)

    def fill(t, carry):
        r = pl.multiple_of(t * pair, pair)
        pad_s[pl.ds(M + r, pair), :] = u_ref[pl.ds(r, pair), :]
        return carry

    lax.fori_loop(0, seq // pair, fill, 0)

    @pl.when(pl.program_id(1) == 0)
    def _():
        d = (lax.broadcasted_iota(jnp.int32, (tile, K), 1)
             - lax.broadcasted_iota(jnp.int32, (tile, K), 0) - M)
        band_s[...] = jnp.where(d >= -hw, jnp.where(d < hw, 1.0, 0.0), 0.0).astype(BF16)

    w = w_ref[0]
    scale = sc_ref[...]

    def body(t, carry):
        r = pl.multiple_of(t * pair, pair)
        rows = pl.ds(r, pair)
        windows = jnp.concatenate([pad_s[pl.ds(r, K), :], pad_s[pl.ds(r + tile, K), :]], axis=1)
        wsum = jnp.dot(band_s[...], windows, preferred_element_type=F32)
        wsum = jnp.concatenate([wsum[:, :POOL_DG], wsum[:, POOL_DG:]], axis=0)
        cnt = cnt_ref[rows, :]
        cnt = jnp.concatenate([cnt] * (POOL_DG // cnt.shape[1]), axis=1)
        p = wsum / cnt - u_ref[rows, :].astype(F32)
        y = jnp.dot(p.astype(BF16), w, preferred_element_type=F32) * scale
        gate = _silu(g_ref[rows, :].astype(F32))
        o_ref[rows, :] = (y * gate).astype(o_ref.dtype)
        return carry

    lax.fori_loop(0, seq // pair, body, 0, unroll=True)


def _pool(zm, pool_w16, pool_scale, layer, batch, seq, tile=256):
    dg = POOL_DG
    n_groups = len(POOL_WINDOWS)
    assert max(POOL_WINDOWS) // 2 <= POOL_MARGIN and seq >= tile + 2 * POOL_MARGIN
    half_windows = jnp.asarray([w // 2 for w in POOL_WINDOWS], jnp.int32)
    pos = jnp.arange(seq, dtype=jnp.int32)[None, :, None]
    hw3 = half_windows[:, None, None]
    counts = (jnp.minimum(pos + hw3, seq) - jnp.maximum(pos - hw3, 0)).astype(F32)
    counts = jnp.broadcast_to(counts, (n_groups, seq, 128))
    return pl.pallas_call(
        functools.partial(_pool_kernel, seq=seq, tile=tile),
        grid=(n_groups, batch),
        in_specs=[
            pl.BlockSpec(memory_space=pltpu.SMEM),
            pl.BlockSpec((seq, dg), lambda g, b: (b, ZM_PV // dg + g)),
            pl.BlockSpec((seq, dg), lambda g, b: (b, ZM_PG // dg + g)),
            pl.BlockSpec((None, seq, 128), lambda g, b: (g, 0, 0)),
            pl.BlockSpec((None, 1, dg, dg), lambda g, b: (layer, g, 0, 0)),
            pl.BlockSpec((1, dg), lambda g, b: (0, g)),
        ],
        out_specs=pl.BlockSpec((seq, dg), lambda g, b: (b, g)),
        out_shape=jax.ShapeDtypeStruct((batch * seq, BRANCH), BF16),
        scratch_shapes=[pltpu.VMEM((seq + 2 * POOL_MARGIN, dg), BF16),
                        pltpu.VMEM((tile, tile + 2 * POOL_MARGIN), BF16)],
        compiler_params=_GRID2,
        name="pool",
    )(half_windows, zm, zm, counts, pool_w16, pool_scale)


def _att_kernel(sink_ref, q_ref, k_ref, v_ref, g_ref, o_ref, *, seq, nq):
    L = ATT_BLOCK
    G = ATT_GROUP
    kk = pl.program_id(1)
    n = pl.program_id(2)

    row = lax.broadcasted_iota(jnp.int32, (L, 3 * L), 0)
    col = lax.broadcasted_iota(jnp.int32, (L, 3 * L), 1)
    for blk in range(nq):
        nb = n * nq + blk
        rows = slice(blk * L, (blk + 1) * L)
        q = jnp.concatenate([q_ref[rows, h * ATT_DH:(h + 1) * ATT_DH] for h in range(G)],
                            axis=0)
        start = jnp.clip((nb - 1) * L, 0, seq - 3 * L)
        start = pl.multiple_of(start, L)
        kw = k_ref[pl.ds(start, 3 * L), :]
        vw = jnp.concatenate([v_ref[pl.ds(start, 3 * L), :], jnp.ones((3 * L, ATT_DH), BF16)],
                             axis=1)
        s = lax.dot_general(q, kw, (((1,), (1,)), ((), ())),
                            preferred_element_type=F32)
        valid = jnp.abs(col - row + (start - nb * L)) <= ATT_WINDOW
        outs = []
        for h in range(G):
            sh = jnp.where(valid, s[h * L:(h + 1) * L, :], NEG_BIG)
            sk = jnp.full((L, 1), sink_ref[kk * G + h] * LOG2_E, F32)
            m = jnp.maximum(jnp.max(sh, axis=-1, keepdims=True), sk)
            p = jnp.exp2(sh - m)
            pv = jnp.dot(p.astype(BF16), vw, preferred_element_type=F32)
            denom = pv[:, ATT_DH:] + jnp.exp2(sk - m)
            oh = pv[:, :ATT_DH] / denom
            gate = _silu(g_ref[rows, h * ATT_DH:(h + 1) * ATT_DH].astype(F32))
            outs.append((oh * gate).astype(o_ref.dtype))
        o_ref[rows, :] = jnp.concatenate(outs, axis=1)


def _attention(za, zm, sink, batch, seq, nq=32):
    L = ATT_BLOCK
    n_blocks = seq // L
    assert n_blocks % nq == 0
    n_steps = n_blocks // nq
    gw = ATT_GROUP * ATT_DH
    smem = pl.BlockSpec(memory_space=pltpu.SMEM)
    return pl.pallas_call(
        functools.partial(_att_kernel, seq=seq, nq=nq),
        grid=(batch, ATT_KV_HEADS, n_steps),
        in_specs=[
            smem,
            pl.BlockSpec((nq * L, gw), lambda b, k, n: (b * n_steps + n, k)),
            pl.BlockSpec((seq, ATT_DH), lambda b, k, n: (b, ZA_K // ATT_DH + k)),
            pl.BlockSpec((seq, ATT_DH), lambda b, k, n: (b, ZA_V // ATT_DH + k)),
            pl.BlockSpec((nq * L, gw), lambda b, k, n: (b * n_steps + n, ZM_AG // gw + k)),
        ],
        out_specs=pl.BlockSpec((nq * L, gw), lambda b, k, n: (b * n_steps + n, k)),
        out_shape=jax.ShapeDtypeStruct((batch * seq, BRANCH), BF16),
        compiler_params=pltpu.CompilerParams(
            dimension_semantics=("arbitrary", "arbitrary", "arbitrary"),
            vmem_limit_bytes=VMEM_LIMIT),
        name="attention",
    )(sink, za, za, za, zm)


def _merge_kernel(x_ref, ya_ref, yb_ref, yc_ref, *rest, n_col, emit_next):
    gate_refs = rest[:N_GATES * n_col]
    rest = rest[N_GATES * n_col:]
    if emit_next:
        wa_ref, wb_ref, wc_ref, wo_ref, gn_ref, o_ref, hn_ref, m_s = rest
    else:
        wa_ref, wb_ref, wc_ref, wo_ref, o_ref, m_s = rest
    tc = GATE_BLOCK
    ya = ya_ref[...]
    yb = yb_ref[...]
    yc = yc_ref[...]
    for c in range(n_col):
        cs = slice(c * tc, (c + 1) * tc)
        a = jnp.dot(ya, wa_ref[:, cs], preferred_element_type=F32)
        b = jnp.dot(yb, wb_ref[:, cs], preferred_element_type=F32)
        cc = jnp.dot(yc, wc_ref[:, cs], preferred_element_type=F32)
        g0, g1, g2 = (jax.nn.sigmoid(gate_refs[j * n_col + c][...].astype(F32))
                      for j in range(N_GATES))
        m_s[:, cs] = (g0 * a + g1 * b + g2 * cc).astype(BF16)
    x_new = x_ref[...] + jnp.dot(m_s[...], wo_ref[...], preferred_element_type=F32)
    o_ref[...] = x_new
    if emit_next:
        hn_ref[...] = _rms_norm_rows(x_new, gn_ref[...]).astype(hn_ref.dtype)


def _merge(x2, ya, yb, yc, zm, wa, wb, wc, wo, layer, g_next, tm=256):
    m, d = x2.shape
    n_col = d // GATE_BLOCK
    emit_next = g_next is not None
    row = lambda w: pl.BlockSpec((tm, w), lambda i: (i, 0))
    gate = lambda j, c: pl.BlockSpec(
        (tm, GATE_BLOCK), lambda i, o=ZM_MG // GATE_BLOCK + j * n_col + c: (i, o))
    gates = [gate(j, c) for j in range(N_GATES) for c in range(n_col)]
    wspec = lambda k: pl.BlockSpec((None, k, d), lambda i: (layer, 0, 0),
                                   pipeline_mode=pl.Buffered(1))
    in_specs = ([row(d), row(BRANCH), row(BRANCH), row(BRANCH)] + gates
                + [wspec(BRANCH), wspec(BRANCH), wspec(BRANCH), wspec(d)])
    args = [x2, ya, yb, yc] + [zm] * len(gates) + [wa, wb, wc, wo]
    out_specs = row(d)
    out_shape = jax.ShapeDtypeStruct((m, d), F32)
    if emit_next:
        in_specs.append(pl.BlockSpec((1, d), lambda i: (0, 0)))
        args.append(g_next)
        out_specs = (out_specs, row(d))
        out_shape = (out_shape, jax.ShapeDtypeStruct((m, d), BF16))
    return pl.pallas_call(
        functools.partial(_merge_kernel, n_col=n_col, emit_next=emit_next),
        grid=(m // tm,),
        in_specs=in_specs,
        out_specs=out_specs,
        out_shape=out_shape,
        scratch_shapes=[pltpu.VMEM((tm, d), BF16)],
        compiler_params=pltpu.CompilerParams(
            dimension_semantics=("arbitrary",),
            vmem_limit_bytes=VMEM_LIMIT),
        name="merge",
    )(*args)


def _ret_tables(seq):
    half = RET_DH // 2
    inv = 1.0 / (RET_ROPE_BASE ** jnp.linspace(0.0, 1.0, half, dtype=F32))
    ang = jnp.arange(seq, dtype=F32)[:, None] * inv[None, :]
    return jnp.cos(ang), jnp.sin(ang)


def _att_tables(seq):
    half = ROPE_DIMS // 2
    inv = ROPE_THETA ** (-jnp.arange(half, dtype=F32) / half)
    ang = jnp.arange(seq, dtype=F32)[:, None] * inv[None, :]
    c, s = jnp.cos(ang), jnp.sin(ang)
    rest = ATT_DH - ROPE_DIMS
    cos = jnp.concatenate([c, c, jnp.ones((seq, rest), F32)], axis=1)
    sin = jnp.concatenate([-s, s, jnp.zeros((seq, rest), F32)], axis=1)
    return cos, sin


def kernel(x, norm_g, w_in, ret_decay_fwd, ret_decay_bwd, pool_w, pool_scale,
           attn_q_gain, attn_k_gain, attn_sink, w_ret, w_pool, w_att, w_out):
    batch, seq, d = x.shape
    depth = w_in.shape[0]
    assert d == D_MODEL and w_in.shape[2] == IN_WIDTH
    assert seq >= 3 * ATT_BLOCK
    rcos, rsin = _ret_tables(seq)
    acos, asin = _att_tables(seq)
    w_ret16, w_pool16, w_att16, w_out16, pool_w16 = (
        w.astype(BF16) for w in (w_ret, w_pool, w_att, w_out, pool_w))
    x2 = x.reshape(batch * seq, d)
    h = _norm(x2, norm_g[0][None, :])
    for l in range(depth):
        q_gain = attn_q_gain[l] * (ATT_DH ** -0.5 * LOG2_E)
        qk_gains = jnp.concatenate(
            [jnp.tile(q_gain, ATT_Q_HEADS), jnp.tile(attn_k_gain[l], ATT_KV_HEADS),
             jnp.ones((ATT_KV_WIDTH,), F32)])[None, :]
        za = _proj_att(h, w_in, l, qk_gains, acos, asin, seq)
        zm = _proj_main(h, w_in, l)
        ya = _retention(zm, ret_decay_fwd[l], ret_decay_bwd[l], rcos, rsin, batch, seq)
        yb = _pool(zm, pool_w16, pool_scale[l][None, :], l, batch, seq)
        yc = _attention(za, zm, attn_sink[l], batch, seq)
        g_next = norm_g[l + 1][None, :] if l + 1 < depth else None
        res = _merge(x2, ya, yb, yc, zm, w_ret16, w_pool16, w_att16, w_out16, l, g_next)
        x2, h = res if g_next is not None else (res, None)
    return x2.reshape(batch, seq, d)
```
